```python
import math
import jax, jax.numpy as jnp
from jax import lax
import numpy as np

D_MODEL = 1024
BATCH = 8
SEQ = 16384
DEPTH = 2

CHUNK = 64
SSM_GROUP = 16
SSM_GROUPS = 16
D_SSM = SSM_GROUP * SSM_GROUPS
SSM_STATE = 64
SB_HEADS = 8
SB_HEAD_DIM = 64
D_SB = SB_HEADS * SB_HEAD_DIM
D_CONV = 256
CONV_WIDTH = 3
D_FF = 4 * D_MODEL
QBLOCK = 128
EPS = 1e-6
DT_MIN = 1e-3
DT_MAX = 1e-1
N_BRANCH = 3
D_IN = D_SSM + 3 * D_SB + 3 * D_CONV
SPLITS = [D_SSM, D_SSM + D_SB, D_SSM + 2 * D_SB, D_SSM + 3 * D_SB,
          D_SSM + 3 * D_SB + D_CONV, D_SSM + 3 * D_SB + 2 * D_CONV]

kernel_name = "hybrid_s5_stickbreak_shortconv_block"


def rmsnorm(x, g):
    x32 = x.astype(jnp.float32)
    y = x32 * lax.rsqrt(jnp.mean(x32 * x32, axis=-1, keepdims=True) + EPS) * g.astype(jnp.float32)
    return y.astype(x.dtype)


def _complex_linear_combine(e1, e2):
    a1r, a1i, b1r, b1i = e1
    a2r, a2i, b2r, b2i = e2
    ar = a1r * a2r - a1i * a2i
    ai = a1r * a2i + a1i * a2r
    br = a2r * b1r - a2i * b1i + b2r
    bi = a2r * b1i + a2i * b1r + b2i
    return (ar, ai, br, bi)


def s5_branch(u, a_re, a_im, log_dt, b_re, b_im, c_re, c_im, d_skip, w_val, w_gate):
    bsz, s, _ = u.shape
    f32 = jnp.float32
    uf = u.astype(f32)
    ug = uf.reshape(bsz, s, SSM_GROUPS, SSM_GROUP)
    dt = jnp.exp(log_dt.astype(f32))[:, None]
    lr = a_re.astype(f32)
    li = a_im.astype(f32)
    mag = jnp.exp(lr * dt)
    ab_re = mag * jnp.cos(li * dt)
    ab_im = mag * jnp.sin(li * dt)
    den = lr * lr + li * li
    xr = ab_re - 1.0
    coef_re = (xr * lr + ab_im * li) / den
    coef_im = (ab_im * lr - xr * li) / den
    br = b_re.astype(f32)
    bi = b_im.astype(f32)
    bb_re = coef_re[..., None] * br - coef_im[..., None] * bi
    bb_im = coef_re[..., None] * bi + coef_im[..., None] * br
    bu_re = jnp.einsum('bsgc,gnc->bsgn', ug, bb_re)
    bu_im = jnp.einsum('bsgc,gnc->bsgn', ug, bb_im)
    a_r = jnp.broadcast_to(ab_re, bu_re.shape)
    a_i = jnp.broadcast_to(ab_im, bu_re.shape)
    _, _, h_re, h_im = lax.associative_scan(_complex_linear_combine, (a_r, a_i, bu_re, bu_im), axis=1)
    y = (jnp.einsum('bsgn,gcn->bsgc', h_re, c_re.astype(f32))
         - jnp.einsum('bsgn,gcn->bsgc', h_im, c_im.astype(f32)))
    y = y.reshape(bsz, s, D_SSM) + d_skip.astype(f32) * uf
    y = jax.nn.gelu(y).astype(u.dtype)
    return (y @ w_val) * jax.nn.sigmoid(y @ w_gate)


def stick_breaking_attention(q, k, v):
    s_len = q.shape[2]
    scale = SB_HEAD_DIM ** -0.5
    outs = []
    for i in range(s_len // QBLOCK):
        t0 = i * QBLOCK
        t1 = t0 + QBLOCK
        kb = k[:, :, :t1]
        vb = v[:, :, :t1]
        z = jnp.einsum('bhtd,bhsd->bhts', q[:, :, t0:t1], kb).astype(jnp.float32) * scale
        tpos = t0 + jnp.arange(QBLOCK)[:, None]
        spos = jnp.arange(t1)[None, :]
        mask = spos < tpos
        log_1m = jnp.where(mask, jax.nn.log_sigmoid(-z), 0.0)
        suffix = lax.cumsum(log_1m, axis=log_1m.ndim - 1, reverse=True) - log_1m
        w = jnp.where(mask, jnp.exp(jax.nn.log_sigmoid(z) + suffix), 0.0)
        outs.append(jnp.einsum('bhts,bhsd->bhtd', w.astype(v.dtype), vb))
    return jnp.concatenate(outs, axis=2)


def short_conv_branch(gate_b, gate_c, x_in, conv_w, w_out):
    z = gate_c * x_in
    conv_filter = conv_w[:, None, :].astype(z.dtype)
    y = lax.conv_general_dilated(z, conv_filter, window_strides=(1,),
                                 padding=[(CONV_WIDTH - 1, 0)],
                                 dimension_numbers=('NWC', 'WIO', 'NWC'),
                                 feature_group_count=D_CONV)
    return (gate_b * y) @ w_out


def _fwd_setup_inputs(seed: int = 0) -> dict:
    key = jax.random.key(seed)
    ks = jax.random.split(key, 24)
    f32 = jnp.float32

    def nrm(k, shape, scale):
        return jax.random.normal(k, shape, f32) * scale

    n_idx = jnp.arange(SSM_STATE, dtype=f32)
    return {
        "x": nrm(ks[0], (BATCH, SEQ, D_MODEL), 1.0),
        "norm_mix_pre": 1.0 + nrm(ks[1], (DEPTH, D_MODEL), 0.02),
        "norm_mix_post": 1.0 + nrm(ks[2], (DEPTH, D_MODEL), 0.02),
        "w_in": nrm(ks[3], (DEPTH, D_MODEL, D_IN), D_MODEL ** -0.5),
        "w_gate": nrm(ks[4], (DEPTH, D_MODEL, N_BRANCH * D_MODEL), D_MODEL ** -0.5),
        "b_gate": nrm(ks[5], (DEPTH, N_BRANCH * D_MODEL), 0.01),
        "ssm_a_re": -0.5 + nrm(ks[6], (DEPTH, SSM_GROUPS, SSM_STATE), 0.01),
        "ssm_a_im": math.pi * n_idx + nrm(ks[7], (DEPTH, SSM_GROUPS, SSM_STATE), 0.01),
        "ssm_log_dt": jax.random.uniform(ks[8], (DEPTH, SSM_GROUPS), f32,
                                         math.log(DT_MIN), math.log(DT_MAX)),
        "ssm_b_re": nrm(ks[9], (DEPTH, SSM_GROUPS, SSM_STATE, SSM_GROUP), (2 * SSM_GROUP) ** -0.5),
        "ssm_b_im": nrm(ks[10], (DEPTH, SSM_GROUPS, SSM_STATE, SSM_GROUP), (2 * SSM_GROUP) ** -0.5),
        "ssm_c_re": nrm(ks[11], (DEPTH, SSM_GROUPS, SSM_GROUP, SSM_STATE), SSM_STATE ** -0.5),
        "ssm_c_im": nrm(ks[12], (DEPTH, SSM_GROUPS, SSM_GROUP, SSM_STATE), SSM_STATE ** -0.5),
        "ssm_d": nrm(ks[13], (DEPTH, D_SSM), 1.0),
        "w_glu_val": nrm(ks[14], (DEPTH, D_SSM, D_MODEL), D_SSM ** -0.5),
        "w_glu_gate": nrm(ks[15], (DEPTH, D_SSM, D_MODEL), D_SSM ** -0.5),
        "w_attn_out": nrm(ks[16], (DEPTH, D_SB, D_MODEL), D_SB ** -0.5),
        "conv_w": nrm(ks[17], (DEPTH, CONV_WIDTH, D_CONV), CONV_WIDTH ** -0.5),
        "w_conv_out": nrm(ks[18], (DEPTH, D_CONV, D_MODEL), D_CONV ** -0.5),
        "w_mix_out": nrm(ks[19], (DEPTH, D_MODEL, D_MODEL), D_MODEL ** -0.5),
        "norm_ffn_pre": 1.0 + nrm(ks[20], (DEPTH, D_MODEL), 0.02),
        "norm_ffn_post": 1.0 + nrm(ks[21], (DEPTH, D_MODEL), 0.02),
        "w_ffn_up": nrm(ks[22], (DEPTH, D_MODEL, D_FF), D_MODEL ** -0.5),
        "w_ffn_down": nrm(ks[23], (DEPTH, D_FF, D_MODEL), D_FF ** -0.5),
    }


def _fwd_reference(x, norm_mix_pre, norm_mix_post, w_in, w_gate, b_gate,
              ssm_a_re, ssm_a_im, ssm_log_dt, ssm_b_re, ssm_b_im, ssm_c_re, ssm_c_im, ssm_d,
              w_glu_val, w_glu_gate, w_attn_out, conv_w, w_conv_out, w_mix_out,
              norm_ffn_pre, norm_ffn_post, w_ffn_up, w_ffn_down):
    bsz, s_len, _ = x.shape

    def heads(t):
        return t.reshape(bsz, s_len, SB_HEADS, SB_HEAD_DIM).transpose(0, 2, 1, 3)

    for l in range(DEPTH):
        h = rmsnorm(x, norm_mix_pre[l])
        p = h @ w_in[l]
        u, q, k, v, cb, cc, cx = jnp.split(p, SPLITS, axis=-1)

        y_ssm = s5_branch(u, ssm_a_re[l], ssm_a_im[l], ssm_log_dt[l], ssm_b_re[l], ssm_b_im[l],
                          ssm_c_re[l], ssm_c_im[l], ssm_d[l], w_glu_val[l], w_glu_gate[l])
        o = stick_breaking_attention(heads(q), heads(k), heads(v))
        y_sb = o.transpose(0, 2, 1, 3).reshape(bsz, s_len, D_SB) @ w_attn_out[l]
        y_conv = short_conv_branch(cb, cc, cx, conv_w[l], w_conv_out[l])

        g_ssm, g_sb, g_conv = jnp.split(jax.nn.sigmoid(h @ w_gate[l] + b_gate[l]), N_BRANCH, axis=-1)
        merged = g_ssm * y_ssm + g_sb * y_sb + g_conv * y_conv
        x = x + rmsnorm(merged @ w_mix_out[l], norm_mix_post[l])

        h = rmsnorm(x, norm_ffn_pre[l])
        f = jnp.square(jax.nn.relu(h @ w_ffn_up[l])) @ w_ffn_down[l]
        x = x + rmsnorm(f, norm_ffn_post[l])
    return x


import jax as _jax
import jax.numpy as _jnp

TWIN_FORMAT = 'train_step'
FWD_PARAMS = ['x', 'norm_mix_pre', 'norm_mix_post', 'w_in', 'w_gate', 'b_gate', 'ssm_a_re', 'ssm_a_im', 'ssm_log_dt', 'ssm_b_re', 'ssm_b_im', 'ssm_c_re', 'ssm_c_im', 'ssm_d', 'w_glu_val', 'w_glu_gate', 'w_attn_out', 'conv_w', 'w_conv_out', 'w_mix_out', 'norm_ffn_pre', 'norm_ffn_post', 'w_ffn_up', 'w_ffn_down']
TWIN_WEIGHTS = ['norm_mix_pre', 'norm_mix_post', 'w_in', 'w_gate', 'b_gate', 'ssm_a_re', 'ssm_a_im', 'ssm_log_dt', 'ssm_b_re', 'ssm_b_im', 'ssm_c_re', 'ssm_c_im', 'ssm_d', 'w_glu_val', 'w_glu_gate', 'w_attn_out', 'conv_w', 'w_conv_out', 'w_mix_out', 'norm_ffn_pre', 'norm_ffn_post', 'w_ffn_up', 'w_ffn_down']
TWIN_DIFF_INPUT = 'x'
TWIN_INPUTS = ['x', 'norm_mix_pre', 'norm_mix_post', 'w_in', 'w_gate', 'b_gate', 'ssm_a_re', 'ssm_a_im', 'ssm_log_dt', 'ssm_b_re', 'ssm_b_im', 'ssm_c_re', 'ssm_c_im', 'ssm_d', 'w_glu_val', 'w_glu_gate', 'w_attn_out', 'conv_w', 'w_conv_out', 'w_mix_out', 'norm_ffn_pre', 'norm_ffn_post', 'w_ffn_up', 'w_ffn_down', 'loss_target', 'm_norm_mix_pre', 'm_norm_mix_post', 'm_w_in', 'm_w_gate', 'm_b_gate', 'm_ssm_a_re', 'm_ssm_a_im', 'm_ssm_log_dt', 'm_ssm_b_re', 'm_ssm_b_im', 'm_ssm_c_re', 'm_ssm_c_im', 'm_ssm_d', 'm_w_glu_val', 'm_w_glu_gate', 'm_w_attn_out', 'm_conv_w', 'm_w_conv_out', 'm_w_mix_out', 'm_norm_ffn_pre', 'm_norm_ffn_post', 'm_w_ffn_up', 'm_w_ffn_down', 'v_norm_mix_pre', 'v_norm_mix_post', 'v_w_in', 'v_w_gate', 'v_b_gate', 'v_ssm_a_re', 'v_ssm_a_im', 'v_ssm_log_dt', 'v_ssm_b_re', 'v_ssm_b_im', 'v_ssm_c_re', 'v_ssm_c_im', 'v_ssm_d', 'v_w_glu_val', 'v_w_glu_gate', 'v_w_attn_out', 'v_conv_w', 'v_w_conv_out', 'v_w_mix_out', 'v_norm_ffn_pre', 'v_norm_ffn_post', 'v_w_ffn_up', 'v_w_ffn_down']
TWIN_OUTPUTS = ['loss', 'grad_x', 'grad_norm_mix_pre', 'grad_norm_mix_post', 'grad_w_in', 'grad_w_gate', 'grad_b_gate', 'grad_ssm_a_re', 'grad_ssm_a_im', 'grad_ssm_log_dt', 'grad_ssm_b_re', 'grad_ssm_b_im', 'grad_ssm_c_re', 'grad_ssm_c_im', 'grad_ssm_d', 'grad_w_glu_val', 'grad_w_glu_gate', 'grad_w_attn_out', 'grad_conv_w', 'grad_w_conv_out', 'grad_w_mix_out', 'grad_norm_ffn_pre', 'grad_norm_ffn_post', 'grad_w_ffn_up', 'grad_w_ffn_down', 'delta_norm_mix_pre', 'delta_norm_mix_post', 'delta_w_in', 'delta_w_gate', 'delta_b_gate', 'delta_ssm_a_re', 'delta_ssm_a_im', 'delta_ssm_log_dt', 'delta_ssm_b_re', 'delta_ssm_b_im', 'delta_ssm_c_re', 'delta_ssm_c_im', 'delta_ssm_d', 'delta_w_glu_val', 'delta_w_glu_gate', 'delta_w_attn_out', 'delta_conv_w', 'delta_w_conv_out', 'delta_w_mix_out', 'delta_norm_ffn_pre', 'delta_norm_ffn_post', 'delta_w_ffn_up', 'delta_w_ffn_down', 'new_m_norm_mix_pre', 'new_m_norm_mix_post', 'new_m_w_in', 'new_m_w_gate', 'new_m_b_gate', 'new_m_ssm_a_re', 'new_m_ssm_a_im', 'new_m_ssm_log_dt', 'new_m_ssm_b_re', 'new_m_ssm_b_im', 'new_m_ssm_c_re', 'new_m_ssm_c_im', 'new_m_ssm_d', 'new_m_w_glu_val', 'new_m_w_glu_gate', 'new_m_w_attn_out', 'new_m_conv_w', 'new_m_w_conv_out', 'new_m_w_mix_out', 'new_m_norm_ffn_pre', 'new_m_norm_ffn_post', 'new_m_w_ffn_up', 'new_m_w_ffn_down', 'new_v_norm_mix_pre', 'new_v_norm_mix_post', 'new_v_w_in', 'new_v_w_gate', 'new_v_b_gate', 'new_v_ssm_a_re', 'new_v_ssm_a_im', 'new_v_ssm_log_dt', 'new_v_ssm_b_re', 'new_v_ssm_b_im', 'new_v_ssm_c_re', 'new_v_ssm_c_im', 'new_v_ssm_d', 'new_v_w_glu_val', 'new_v_w_glu_gate', 'new_v_w_attn_out', 'new_v_conv_w', 'new_v_w_conv_out', 'new_v_w_mix_out', 'new_v_norm_ffn_pre', 'new_v_norm_ffn_post', 'new_v_w_ffn_up', 'new_v_w_ffn_down']
TWIN_LEAF_KINDS = {'loss': 'loss', 'grad_x': 'grad_x', 'grad_norm_mix_pre': 'grad_w', 'grad_norm_mix_post': 'grad_w', 'grad_w_in': 'grad_w', 'grad_w_gate': 'grad_w', 'grad_b_gate': 'grad_w', 'grad_ssm_a_re': 'grad_w', 'grad_ssm_a_im': 'grad_w', 'grad_ssm_log_dt': 'grad_w', 'grad_ssm_b_re': 'grad_w', 'grad_ssm_b_im': 'grad_w', 'grad_ssm_c_re': 'grad_w', 'grad_ssm_c_im': 'grad_w', 'grad_ssm_d': 'grad_w', 'grad_w_glu_val': 'grad_w', 'grad_w_glu_gate': 'grad_w', 'grad_w_attn_out': 'grad_w', 'grad_conv_w': 'grad_w', 'grad_w_conv_out': 'grad_w', 'grad_w_mix_out': 'grad_w', 'grad_norm_ffn_pre': 'grad_w', 'grad_norm_ffn_post': 'grad_w', 'grad_w_ffn_up': 'grad_w', 'grad_w_ffn_down': 'grad_w', 'delta_norm_mix_pre': 'delta_w', 'delta_norm_mix_post': 'delta_w', 'delta_w_in': 'delta_w', 'delta_w_gate': 'delta_w', 'delta_b_gate': 'delta_w', 'delta_ssm_a_re': 'delta_w', 'delta_ssm_a_im': 'delta_w', 'delta_ssm_log_dt': 'delta_w', 'delta_ssm_b_re': 'delta_w', 'delta_ssm_b_im': 'delta_w', 'delta_ssm_c_re': 'delta_w', 'delta_ssm_c_im': 'delta_w', 'delta_ssm_d': 'delta_w', 'delta_w_glu_val': 'delta_w', 'delta_w_glu_gate': 'delta_w', 'delta_w_attn_out': 'delta_w', 'delta_conv_w': 'delta_w', 'delta_w_conv_out': 'delta_w', 'delta_w_mix_out': 'delta_w', 'delta_norm_ffn_pre': 'delta_w', 'delta_norm_ffn_post': 'delta_w', 'delta_w_ffn_up': 'delta_w', 'delta_w_ffn_down': 'delta_w', 'new_m_norm_mix_pre': 'new_m', 'new_m_norm_mix_post': 'new_m', 'new_m_w_in': 'new_m', 'new_m_w_gate': 'new_m', 'new_m_b_gate': 'new_m', 'new_m_ssm_a_re': 'new_m', 'new_m_ssm_a_im': 'new_m', 'new_m_ssm_log_dt': 'new_m', 'new_m_ssm_b_re': 'new_m', 'new_m_ssm_b_im': 'new_m', 'new_m_ssm_c_re': 'new_m', 'new_m_ssm_c_im': 'new_m', 'new_m_ssm_d': 'new_m', 'new_m_w_glu_val': 'new_m', 'new_m_w_glu_gate': 'new_m', 'new_m_w_attn_out': 'new_m', 'new_m_conv_w': 'new_m', 'new_m_w_conv_out': 'new_m', 'new_m_w_mix_out': 'new_m', 'new_m_norm_ffn_pre': 'new_m', 'new_m_norm_ffn_post': 'new_m', 'new_m_w_ffn_up': 'new_m', 'new_m_w_ffn_down': 'new_m', 'new_v_norm_mix_pre': 'new_v', 'new_v_norm_mix_post': 'new_v', 'new_v_w_in': 'new_v', 'new_v_w_gate': 'new_v', 'new_v_b_gate': 'new_v', 'new_v_ssm_a_re': 'new_v', 'new_v_ssm_a_im': 'new_v', 'new_v_ssm_log_dt': 'new_v', 'new_v_ssm_b_re': 'new_v', 'new_v_ssm_b_im': 'new_v', 'new_v_ssm_c_re': 'new_v', 'new_v_ssm_c_im': 'new_v', 'new_v_ssm_d': 'new_v', 'new_v_w_glu_val': 'new_v', 'new_v_w_glu_gate': 'new_v', 'new_v_w_attn_out': 'new_v', 'new_v_conv_w': 'new_v', 'new_v_w_conv_out': 'new_v', 'new_v_w_mix_out': 'new_v', 'new_v_norm_ffn_pre': 'new_v', 'new_v_norm_ffn_post': 'new_v', 'new_v_w_ffn_up': 'new_v', 'new_v_w_ffn_down': 'new_v'}


def _forward(args):
    return _fwd_reference(*[args[k] for k in FWD_PARAMS])


def _output_shape():
    def fwd():
        inp = _fwd_setup_inputs(0)
        return _fwd_reference(*[inp[k] for k in FWD_PARAMS])
    out = _jax.eval_shape(fwd)
    return out.shape, out.dtype

N_MICROBATCH = 1
ADAM_LR = 0.001
ADAM_B1 = 0.9
ADAM_B2 = 0.999
ADAM_EPS = 1e-08
ADAM_WD = 0.01
ADAM_STEP = 10
PER_EXAMPLE_BATCH_AXIS = {'x': 0, 'loss_target': 0}
SHARED_INPUTS = []
_WEIGHT_DTYPES = {'norm_mix_pre': _jnp.float32, 'norm_mix_post': _jnp.float32, 'w_in': _jnp.float32, 'w_gate': _jnp.float32, 'b_gate': _jnp.float32, 'ssm_a_re': _jnp.float32, 'ssm_a_im': _jnp.float32, 'ssm_log_dt': _jnp.float32, 'ssm_b_re': _jnp.float32, 'ssm_b_im': _jnp.float32, 'ssm_c_re': _jnp.float32, 'ssm_c_im': _jnp.float32, 'ssm_d': _jnp.float32, 'w_glu_val': _jnp.float32, 'w_glu_gate': _jnp.float32, 'w_attn_out': _jnp.float32, 'conv_w': _jnp.float32, 'w_conv_out': _jnp.float32, 'w_mix_out': _jnp.float32, 'norm_ffn_pre': _jnp.float32, 'norm_ffn_post': _jnp.float32, 'w_ffn_up': _jnp.float32, 'w_ffn_down': _jnp.float32}
MOMENT_SCALE = {'norm_mix_pre': 7.628377e+00, 'norm_mix_post': 1.296447e+02, 'w_in': 4.756628e+00, 'w_gate': 5.746335e-01, 'b_gate': 3.794399e+00, 'ssm_a_re': 3.746996e-01, 'ssm_a_im': 3.221124e-01, 'ssm_log_dt': 6.881386e+01, 'ssm_b_re': 4.285633e-01, 'ssm_b_im': 3.860114e-01, 'ssm_c_re': 6.770424e-01, 'ssm_c_im': 4.910373e-01, 'ssm_d': 2.574004e+01, 'w_glu_val': 1.247657e+01, 'w_glu_gate': 1.948634e+00, 'w_attn_out': 1.069997e+01, 'conv_w': 3.534636e+00, 'w_conv_out': 2.378641e+00, 'w_mix_out': 1.560971e+01, 'norm_ffn_pre': 8.809032e+00, 'norm_ffn_post': 1.380955e+02, 'w_ffn_up': 4.220901e+00, 'w_ffn_down': 3.105717e+01}


def _to_microbatches(a, axis):
    t = _jnp.moveaxis(a, axis, 0)
    t = t.reshape((N_MICROBATCH, t.shape[0] // N_MICROBATCH) + t.shape[1:])
    return _jnp.moveaxis(t, 1, axis + 1)


def setup_inputs(seed: int = 0) -> dict:
    inp = _fwd_setup_inputs(seed)
    key = _jax.random.fold_in(_jax.random.key(seed), 7919)
    shape, _ = _output_shape()
    out = dict(inp)
    out["loss_target"] = _jax.random.normal(_jax.random.fold_in(key, 0), shape, _jnp.float32)
    for i, name in enumerate(TWIN_WEIGHTS):
        w = inp[name].astype(_jnp.float32)
        if MOMENT_SCALE is None:
            s = _jnp.sqrt(_jnp.mean(_jnp.square(w)) + 1e-30)
        else:
            s = MOMENT_SCALE[name]
        km, kv = _jax.random.split(_jax.random.fold_in(key, i + 1))
        out[name] = w
        out["m_" + name] = s * _jax.random.normal(km, w.shape, _jnp.float32)
        out["v_" + name] = (s * s) * _jax.random.uniform(kv, w.shape, _jnp.float32, 0.5, 1.5)
    if N_MICROBATCH > 1:
        for name, axis in PER_EXAMPLE_BATCH_AXIS.items():
            out[name] = _to_microbatches(out[name], axis)
    return {'x': out['x'], 'norm_mix_pre': out['norm_mix_pre'], 'norm_mix_post': out['norm_mix_post'], 'w_in': out['w_in'], 'w_gate': out['w_gate'], 'b_gate': out['b_gate'], 'ssm_a_re': out['ssm_a_re'], 'ssm_a_im': out['ssm_a_im'], 'ssm_log_dt': out['ssm_log_dt'], 'ssm_b_re': out['ssm_b_re'], 'ssm_b_im': out['ssm_b_im'], 'ssm_c_re': out['ssm_c_re'], 'ssm_c_im': out['ssm_c_im'], 'ssm_d': out['ssm_d'], 'w_glu_val': out['w_glu_val'], 'w_glu_gate': out['w_glu_gate'], 'w_attn_out': out['w_attn_out'], 'conv_w': out['conv_w'], 'w_conv_out': out['w_conv_out'], 'w_mix_out': out['w_mix_out'], 'norm_ffn_pre': out['norm_ffn_pre'], 'norm_ffn_post': out['norm_ffn_post'], 'w_ffn_up': out['w_ffn_up'], 'w_ffn_down': out['w_ffn_down'], 'loss_target': out['loss_target'], 'm_norm_mix_pre': out['m_norm_mix_pre'], 'm_norm_mix_post': out['m_norm_mix_post'], 'm_w_in': out['m_w_in'], 'm_w_gate': out['m_w_gate'], 'm_b_gate': out['m_b_gate'], 'm_ssm_a_re': out['m_ssm_a_re'], 'm_ssm_a_im': out['m_ssm_a_im'], 'm_ssm_log_dt': out['m_ssm_log_dt'], 'm_ssm_b_re': out['m_ssm_b_re'], 'm_ssm_b_im': out['m_ssm_b_im'], 'm_ssm_c_re': out['m_ssm_c_re'], 'm_ssm_c_im': out['m_ssm_c_im'], 'm_ssm_d': out['m_ssm_d'], 'm_w_glu_val': out['m_w_glu_val'], 'm_w_glu_gate': out['m_w_glu_gate'], 'm_w_attn_out': out['m_w_attn_out'], 'm_conv_w': out['m_conv_w'], 'm_w_conv_out': out['m_w_conv_out'], 'm_w_mix_out': out['m_w_mix_out'], 'm_norm_ffn_pre': out['m_norm_ffn_pre'], 'm_norm_ffn_post': out['m_norm_ffn_post'], 'm_w_ffn_up': out['m_w_ffn_up'], 'm_w_ffn_down': out['m_w_ffn_down'], 'v_norm_mix_pre': out['v_norm_mix_pre'], 'v_norm_mix_post': out['v_norm_mix_post'], 'v_w_in': out['v_w_in'], 'v_w_gate': out['v_w_gate'], 'v_b_gate': out['v_b_gate'], 'v_ssm_a_re': out['v_ssm_a_re'], 'v_ssm_a_im': out['v_ssm_a_im'], 'v_ssm_log_dt': out['v_ssm_log_dt'], 'v_ssm_b_re': out['v_ssm_b_re'], 'v_ssm_b_im': out['v_ssm_b_im'], 'v_ssm_c_re': out['v_ssm_c_re'], 'v_ssm_c_im': out['v_ssm_c_im'], 'v_ssm_d': out['v_ssm_d'], 'v_w_glu_val': out['v_w_glu_val'], 'v_w_glu_gate': out['v_w_glu_gate'], 'v_w_attn_out': out['v_w_attn_out'], 'v_conv_w': out['v_conv_w'], 'v_w_conv_out': out['v_w_conv_out'], 'v_w_mix_out': out['v_w_mix_out'], 'v_norm_ffn_pre': out['v_norm_ffn_pre'], 'v_norm_ffn_post': out['v_norm_ffn_post'], 'v_w_ffn_up': out['v_w_ffn_up'], 'v_w_ffn_down': out['v_w_ffn_down']}


def _loss(weights, diff, rest, loss_target):
    with _jax.named_scope("forward"):
        args = {**rest, TWIN_DIFF_INPUT: diff, **{k: w.astype(_WEIGHT_DTYPES[k]) for k, w in weights.items()}}
        y = _forward(args)
    with _jax.named_scope("loss_head"):
        err = _jnp.square(y.astype(_jnp.float32) - loss_target)
        return 0.5 * _jnp.sum(_jnp.mean(err, axis=-1)) if err.ndim else 0.5 * err


def _adamw(w, g, m, v):
    m = ADAM_B1 * m + (1.0 - ADAM_B1) * g
    v = ADAM_B2 * v + (1.0 - ADAM_B2) * _jnp.square(g)
    m_hat = m / (1.0 - ADAM_B1 ** ADAM_STEP)
    v_hat = v / (1.0 - ADAM_B2 ** ADAM_STEP)
    delta = -ADAM_LR * (m_hat / (_jnp.sqrt(v_hat) + ADAM_EPS) + ADAM_WD * w)
    return delta, m, v


def reference(x, norm_mix_pre, norm_mix_post, w_in, w_gate, b_gate, ssm_a_re, ssm_a_im, ssm_log_dt, ssm_b_re, ssm_b_im, ssm_c_re, ssm_c_im, ssm_d, w_glu_val, w_glu_gate, w_attn_out, conv_w, w_conv_out, w_mix_out, norm_ffn_pre, norm_ffn_post, w_ffn_up, w_ffn_down, loss_target, m_norm_mix_pre, m_norm_mix_post, m_w_in, m_w_gate, m_b_gate, m_ssm_a_re, m_ssm_a_im, m_ssm_log_dt, m_ssm_b_re, m_ssm_b_im, m_ssm_c_re, m_ssm_c_im, m_ssm_d, m_w_glu_val, m_w_glu_gate, m_w_attn_out, m_conv_w, m_w_conv_out, m_w_mix_out, m_norm_ffn_pre, m_norm_ffn_post, m_w_ffn_up, m_w_ffn_down, v_norm_mix_pre, v_norm_mix_post, v_w_in, v_w_gate, v_b_gate, v_ssm_a_re, v_ssm_a_im, v_ssm_log_dt, v_ssm_b_re, v_ssm_b_im, v_ssm_c_re, v_ssm_c_im, v_ssm_d, v_w_glu_val, v_w_glu_gate, v_w_attn_out, v_conv_w, v_w_conv_out, v_w_mix_out, v_norm_ffn_pre, v_norm_ffn_post, v_w_ffn_up, v_w_ffn_down):
    given = dict(x=x, norm_mix_pre=norm_mix_pre, norm_mix_post=norm_mix_post, w_in=w_in, w_gate=w_gate, b_gate=b_gate, ssm_a_re=ssm_a_re, ssm_a_im=ssm_a_im, ssm_log_dt=ssm_log_dt, ssm_b_re=ssm_b_re, ssm_b_im=ssm_b_im, ssm_c_re=ssm_c_re, ssm_c_im=ssm_c_im, ssm_d=ssm_d, w_glu_val=w_glu_val, w_glu_gate=w_glu_gate, w_attn_out=w_attn_out, conv_w=conv_w, w_conv_out=w_conv_out, w_mix_out=w_mix_out, norm_ffn_pre=norm_ffn_pre, norm_ffn_post=norm_ffn_post, w_ffn_up=w_ffn_up, w_ffn_down=w_ffn_down, loss_target=loss_target, m_norm_mix_pre=m_norm_mix_pre, m_norm_mix_post=m_norm_mix_post, m_w_in=m_w_in, m_w_gate=m_w_gate, m_b_gate=m_b_gate, m_ssm_a_re=m_ssm_a_re, m_ssm_a_im=m_ssm_a_im, m_ssm_log_dt=m_ssm_log_dt, m_ssm_b_re=m_ssm_b_re, m_ssm_b_im=m_ssm_b_im, m_ssm_c_re=m_ssm_c_re, m_ssm_c_im=m_ssm_c_im, m_ssm_d=m_ssm_d, m_w_glu_val=m_w_glu_val, m_w_glu_gate=m_w_glu_gate, m_w_attn_out=m_w_attn_out, m_conv_w=m_conv_w, m_w_conv_out=m_w_conv_out, m_w_mix_out=m_w_mix_out, m_norm_ffn_pre=m_norm_ffn_pre, m_norm_ffn_post=m_norm_ffn_post, m_w_ffn_up=m_w_ffn_up, m_w_ffn_down=m_w_ffn_down, v_norm_mix_pre=v_norm_mix_pre, v_norm_mix_post=v_norm_mix_post, v_w_in=v_w_in, v_w_gate=v_w_gate, v_b_gate=v_b_gate, v_ssm_a_re=v_ssm_a_re, v_ssm_a_im=v_ssm_a_im, v_ssm_log_dt=v_ssm_log_dt, v_ssm_b_re=v_ssm_b_re, v_ssm_b_im=v_ssm_b_im, v_ssm_c_re=v_ssm_c_re, v_ssm_c_im=v_ssm_c_im, v_ssm_d=v_ssm_d, v_w_glu_val=v_w_glu_val, v_w_glu_gate=v_w_glu_gate, v_w_attn_out=v_w_attn_out, v_conv_w=v_conv_w, v_w_conv_out=v_w_conv_out, v_w_mix_out=v_w_mix_out, v_norm_ffn_pre=v_norm_ffn_pre, v_norm_ffn_post=v_norm_ffn_post, v_w_ffn_up=v_w_ffn_up, v_w_ffn_down=v_w_ffn_down)
    weights = {n: given[n] for n in TWIN_WEIGHTS}
    shared = {n: given[n] for n in SHARED_INPUTS}
    per_example = {n: given[n] for n in ['x']}
    grad_fn = _jax.value_and_grad(_loss, argnums=(0, 1))

    def one_microbatch(ex, loss_target):
        ex = dict(ex)
        diff = ex.pop(TWIN_DIFF_INPUT)
        return grad_fn(weights, diff, {**shared, **ex}, loss_target)

    if N_MICROBATCH == 1:
        loss, (grad_w, grad_x) = one_microbatch(per_example, given["loss_target"])
    else:
        def body(carry, xs):
            loss_sum, grad_sum = carry
            l_k, (gw_k, gx_k) = one_microbatch(xs[0], xs[1])
            with _jax.named_scope("update"):
                return (loss_sum + l_k, _jax.tree.map(_jnp.add, grad_sum, gw_k)), gx_k

        init = (_jnp.zeros((), _jnp.float32), _jax.tree.map(_jnp.zeros_like, weights))
        (loss, grad_w), grad_x = _jax.lax.scan(body, init, (per_example, given["loss_target"]))
    with _jax.named_scope("update"):
        delta_w, new_m, new_v = {}, {}, {}
        for n in TWIN_WEIGHTS:
            delta_w[n], new_m[n], new_v[n] = _adamw(weights[n], grad_w[n], given["m_" + n], given["v_" + n])
    return (loss, grad_x, *[grad_w[n] for n in TWIN_WEIGHTS], *[delta_w[n] for n in TWIN_WEIGHTS],
            *[new_m[n] for n in TWIN_WEIGHTS], *[new_v[n] for n in TWIN_WEIGHTS])
```

```python
import functools

import jax
import jax.numpy as jnp
from jax import lax
from jax.experimental import pallas as pl
from jax.experimental.pallas import tpu as pltpu

F32, BF16 = jnp.float32, jnp.bfloat16

D_MODEL = 1024
DEPTH = 2
SSM_GROUPS, SSM_GROUP, SSM_STATE = 16, 16, 64
D_SSM = SSM_GROUPS * SSM_GROUP
N_STATE = SSM_GROUPS * SSM_STATE
SB_HEADS, SB_HEAD_DIM = 8, 64
D_SB = SB_HEADS * SB_HEAD_DIM
D_CONV = 256
D_IN = D_SSM + 3 * D_SB + 3 * D_CONV
D_GATE = 3 * D_MODEL
D_FF = 4 * D_MODEL
EPS = 1e-6
SB_SCALE = SB_HEAD_DIM ** -0.5
GELU_C = 0.7978845608028654
GELU_A = 0.044715

ADAM_LR, ADAM_B1, ADAM_B2, ADAM_EPS, ADAM_WD, ADAM_STEP = 0.001, 0.9, 0.999, 1e-08, 0.01, 10

TM = 256
BQ = 256
R_LANES = 128
HALO = 8
VMEM_LIMIT_MB = 56

MESH = pl.DeviceIdType.MESH
WEIGHTS = ['norm_mix_pre', 'norm_mix_post', 'w_in', 'w_gate', 'b_gate', 'ssm_a_re', 'ssm_a_im', 'ssm_log_dt',
           'ssm_b_re', 'ssm_b_im', 'ssm_c_re', 'ssm_c_im', 'ssm_d', 'w_glu_val', 'w_glu_gate', 'w_attn_out', 'conv_w',
           'w_conv_out', 'w_mix_out', 'norm_ffn_pre', 'norm_ffn_post', 'w_ffn_up', 'w_ffn_down']
SHARDED = {'w_in': 2, 'w_gate': 2, 'w_glu_val': 2, 'w_glu_gate': 2, 'w_attn_out': 2, 'conv_w': 2, 'w_conv_out': 2,
           'w_mix_out': 1, 'w_ffn_up': 2, 'w_ffn_down': 1}
REPLICATED = [n for n in WEIGHTS if n not in SHARDED]
N_CHIPS = 4
N_DEV = 8


def _dot(a, b):
    return jnp.dot(a, b, preferred_element_type=F32)


def _dot_tn(a, b):
    return lax.dot_general(a, b, (((0,), (0,)), ((), ())), preferred_element_type=F32)


def _sigmoid(x):
    return 1.0 / (1.0 + jnp.exp(-x))


def _params(sem, vmem_mb=VMEM_LIMIT_MB):
    return pltpu.CompilerParams(dimension_semantics=sem, vmem_limit_bytes=vmem_mb << 20)


def _rows(tm, n):
    return pl.BlockSpec((tm, n), lambda i: (i, 0))


def _whole(shape):
    zeros = (0,) * len(shape)
    return pl.BlockSpec(shape, lambda *_: zeros)


def _sds(shape, dtype):
    return jax.ShapeDtypeStruct(shape, dtype)


def _rms_fwd(x, g):
    r = lax.rsqrt(jnp.mean(x * x, axis=-1, keepdims=True) + EPS)
    return x * r * g


def _rms_bwd(x, g, dy):
    r = lax.rsqrt(jnp.mean(x * x, axis=-1, keepdims=True) + EPS)
    xh = x * r
    dxh = dy * g
    dx = r * (dxh - xh * jnp.mean(dxh * xh, axis=-1, keepdims=True))
    return dx, dy * xh


def _colsum(a):
    return jnp.sum(a, axis=0, keepdims=True)


def _gelu(y):
    return 0.5 * y * (1.0 + jnp.tanh(GELU_C * (y + GELU_A * y * y * y)))


def _gelu_grad(y):
    th = jnp.tanh(GELU_C * (y + GELU_A * y * y * y))
    return 0.5 * (1.0 + th) + 0.5 * y * (1.0 - th * th) * GELU_C * (1.0 + 3.0 * GELU_A * y * y)


def _fwd_in(x, g_pre, w_in, w_gate, b_gate):
    t = x.shape[0]

    def body(x_ref, g_ref, win_ref, wg_ref, bg_ref, hb_ref, u_ref, q_ref, k_ref, v_ref, cb_ref, cc_ref, cx_ref, gate_ref):
        hb = _rms_fwd(x_ref[...], g_ref[...]).astype(BF16)
        hb_ref[...] = hb
        p = _dot(hb, win_ref[...])
        o = 0
        u_ref[...] = p[:, o:o + D_SSM]
        o += D_SSM
        q_ref[...] = (p[:, o:o + D_SB] * SB_SCALE).astype(BF16)
        o += D_SB
        k_ref[...] = p[:, o:o + D_SB].astype(BF16)
        o += D_SB
        v_ref[...] = p[:, o:o + D_SB].astype(BF16)
        o += D_SB
        cb_ref[...] = p[:, o:o + D_CONV]
        o += D_CONV
        cc_ref[...] = p[:, o:o + D_CONV]
        o += D_CONV
        cx_ref[...] = p[:, o:o + D_CONV]
        gate_ref[...] = _sigmoid(_dot(hb, wg_ref[...]) + bg_ref[...])

    outs = (_sds((t, D_MODEL), BF16), _sds((t, D_SSM), F32), _sds((t, D_SB), BF16), _sds((t, D_SB), BF16),
            _sds((t, D_SB), BF16), _sds((t, D_CONV), F32), _sds((t, D_CONV), F32), _sds((t, D_CONV), F32),
            _sds((t, D_GATE), F32))
    return pl.pallas_call(
        body, name="fwd_in", grid=(t // TM,),
        in_specs=[_rows(TM, D_MODEL), _whole((1, D_MODEL)), _whole((D_MODEL, D_IN)), _whole((D_MODEL, D_GATE)),
                  _whole((1, D_GATE))],
        out_specs=tuple(_rows(TM, s.shape[1]) for s in outs), out_shape=outs,
        compiler_params=_params(("parallel",)),
    )(x, g_pre, w_in, w_gate, b_gate)


def _ssm_scan_rows(n_rows, ar, ai, sign, in_re_ref, in_im_ref, out_re_ref, out_im_ref, carry_re_ref, carry_im_ref,
                   reverse):
    def group(gi, carry):
        hr, hi = carry
        g = (n_rows // 8 - 1 - gi) if reverse else gi
        r0 = pl.multiple_of(g * 8, 8)
        cr = in_re_ref[pl.ds(r0, 8), :]
        ci = in_im_ref[pl.ds(r0, 8), :]
        outs_r, outs_i = [None] * 8, [None] * 8
        for kk in range(8):
            k = 7 - kk if reverse else kk
            nr = ar * hr - sign * ai * hi + cr[k:k + 1, :]
            ni = ar * hi + sign * ai * hr + ci[k:k + 1, :]
            hr, hi = nr, ni
            outs_r[k], outs_i[k] = hr, hi
        out_re_ref[pl.ds(r0, 8), :] = jnp.concatenate(outs_r, axis=0)
        out_im_ref[pl.ds(r0, 8), :] = jnp.concatenate(outs_i, axis=0)
        return hr, hi

    hr, hi = lax.fori_loop(0, n_rows // 8, group, (carry_re_ref[...], carry_im_ref[...]))
    carry_re_ref[...] = hr
    carry_im_ref[...] = hi


def _ssm_fwd(u, bb_re, bb_im, ab_re, ab_im, c_re, c_im, d_skip):
    t = u.shape[0]

    def body(u_ref, bre_ref, bim_ref, ar_ref, ai_ref, cre_ref, cim_ref, d_ref, hre_ref, him_ref, y_ref,
             bur_ref, bui_ref, car_ref, cai_ref):
        @pl.when(pl.program_id(0) == 0)
        def _():
            car_ref[...] = jnp.zeros_like(car_ref)
            cai_ref[...] = jnp.zeros_like(cai_ref)

        uv = u_ref[...]
        ub = uv.astype(BF16)
        bur_ref[...] = _dot(ub, bre_ref[...])
        bui_ref[...] = _dot(ub, bim_ref[...])
        _ssm_scan_rows(TM, ar_ref[...], ai_ref[...], 1.0, bur_ref, bui_ref, hre_ref, him_ref, car_ref, cai_ref, False)
        y_ref[...] = (_dot(hre_ref[...].astype(BF16), cre_ref[...]) - _dot(him_ref[...].astype(BF16), cim_ref[...])
                      + d_ref[...] * uv)

    outs = (_sds((t, N_STATE), F32), _sds((t, N_STATE), F32), _sds((t, D_SSM), F32))
    return pl.pallas_call(
        body, name="ssm_fwd", grid=(t // TM,),
        in_specs=[_rows(TM, D_SSM), _whole((D_SSM, N_STATE)), _whole((D_SSM, N_STATE)), _whole((1, N_STATE)),
                  _whole((1, N_STATE)), _whole((N_STATE, D_SSM)), _whole((N_STATE, D_SSM)), _whole((1, D_SSM))],
        out_specs=(_rows(TM, N_STATE), _rows(TM, N_STATE), _rows(TM, D_SSM)), out_shape=outs,
        scratch_shapes=[pltpu.VMEM((TM, N_STATE), F32), pltpu.VMEM((TM, N_STATE), F32),
                        pltpu.VMEM((1, N_STATE), F32), pltpu.VMEM((1, N_STATE), F32)],
        compiler_params=_params(("arbitrary",)),
    )(u, bb_re, bb_im, ab_re, ab_im, c_re, c_im, d_skip)


def _sb_tile(q, kt, tri, upper, r_run, diag):
    z = _dot(q, kt)
    lse = jnp.log(1.0 + jnp.exp(-jnp.abs(z)))
    a = jnp.minimum(z, 0.0) - lse
    b = a - z
    if diag:
        b = jnp.where(tri, b, 0.0)
    b_hi = b.astype(BF16)
    b_lo = (b - b_hi.astype(F32)).astype(BF16)
    s = r_run + _dot(b_hi, upper) + _dot(b_lo, upper)
    w = jnp.exp(a + s)
    if diag:
        w = jnp.where(tri, w, 0.0)
    return a, b, s, w


def _attn_fwd(q_r, k_t, v_r):
    h, t, _ = q_r.shape
    nk = t // BQ
    assert nk <= R_LANES

    def body(q_ref, kt_ref, v_ref, o_ref, rtab_ref):
        i = pl.program_id(1)
        q = q_ref[0]
        rows = lax.broadcasted_iota(jnp.int32, (BQ, BQ), 0)
        cols = lax.broadcasted_iota(jnp.int32, (BQ, BQ), 1)
        tri = rows > cols
        upper = tri.astype(BF16)
        lane = lax.broadcasted_iota(jnp.int32, (BQ, R_LANES), 1)

        def tile(j, r_run, acc, rtab, diag):
            rtab = jnp.where(lane == j, r_run, rtab)
            _, b, _, w = _sb_tile(q, kt_ref[0, j], tri, upper, r_run, diag)
            acc = acc + _dot(w.astype(BF16), v_ref[0, j])
            return r_run + jnp.sum(b, axis=1, keepdims=True), acc, rtab

        init = (jnp.zeros((BQ, 1), F32), jnp.zeros((BQ, SB_HEAD_DIM), F32), jnp.zeros((BQ, R_LANES), F32))
        carry = tile(i, *init, True)
        carry = lax.fori_loop(0, i, lambda jj, c: tile(i - 1 - jj, *c, False), carry)
        o_ref[0] = carry[1]
        rtab_ref[0] = carry[2]

    return pl.pallas_call(
        body, name="attn_fwd", grid=(h, nk),
        in_specs=[pl.BlockSpec((1, BQ, SB_HEAD_DIM), lambda hh, i: (hh, i, 0)),
                  pl.BlockSpec((1, nk, SB_HEAD_DIM, BQ), lambda hh, i: (hh, 0, 0, 0)),
                  pl.BlockSpec((1, nk, BQ, SB_HEAD_DIM), lambda hh, i: (hh, 0, 0, 0))],
        out_specs=(pl.BlockSpec((1, BQ, SB_HEAD_DIM), lambda hh, i: (hh, i, 0)),
                   pl.BlockSpec((1, BQ, R_LANES), lambda hh, i: (hh, i, 0))),
        out_shape=(_sds((h, t, SB_HEAD_DIM), F32), _sds((h, t, R_LANES), F32)),
        compiler_params=_params(("parallel", "parallel")),
    )(q_r, k_t, v_r)


def _merge_fwd(y_pre, o, cb, cc, cx, conv_w, gate, x, w_val, w_glu, w_ao, w_co, w_mo, g_post):
    t = x.shape[0]

    def body(y_ref, o_ref, cb_ref, cc_ref, cx_ref, ccp_ref, cxp_ref, cw_ref, gate_ref, x_ref, wv_ref, wg_ref, wao_ref,
             wco_ref, wmo_ref, gp_ref, x1_ref, mg_ref, m2_ref):
        i = pl.program_id(0)
        ygb = _gelu(y_ref[...]).astype(BF16)
        y_a = _dot(ygb, wv_ref[...]) * _sigmoid(_dot(ygb, wg_ref[...]))
        y_b = _dot(o_ref[...], wao_ref[...])
        yconv = _conv_fwd(cc_ref[...] * cx_ref[...], jnp.where(i > 0, ccp_ref[...] * cxp_ref[...], 0.0), cw_ref[...])[0]
        y_c = _dot((cb_ref[...] * yconv).astype(BF16), wco_ref[...])
        gate_v = gate_ref[...]
        merged = (gate_v[:, :D_MODEL] * y_a + gate_v[:, D_MODEL:2 * D_MODEL] * y_b
                  + gate_v[:, 2 * D_MODEL:] * y_c).astype(BF16)
        mg_ref[...] = merged
        m2 = _dot(merged, wmo_ref[...])
        m2_ref[...] = m2
        x1_ref[...] = x_ref[...] + _rms_fwd(m2, gp_ref[...])

    outs = (_sds((t, D_MODEL), F32), _sds((t, D_MODEL), BF16), _sds((t, D_MODEL), F32))
    return pl.pallas_call(
        body, name="merge_fwd", grid=(t // TM,),
        in_specs=[_rows(TM, D_SSM), _rows(TM, D_SB), _rows(TM, D_CONV), _rows(TM, D_CONV), _rows(TM, D_CONV),
                  _halo_before(D_CONV), _halo_before(D_CONV), _whole((3, D_CONV)), _rows(TM, D_GATE),
                  _rows(TM, D_MODEL), _whole((D_SSM, D_MODEL)), _whole((D_SSM, D_MODEL)), _whole((D_SB, D_MODEL)),
                  _whole((D_CONV, D_MODEL)), _whole((D_MODEL, D_MODEL)), _whole((1, D_MODEL))],
        out_specs=tuple(_rows(TM, D_MODEL) for _ in outs), out_shape=outs,
        compiler_params=_params(("parallel",)),
    )(y_pre, o, cb, cc, cx, cc, cx, conv_w, gate, x, w_val, w_glu, w_ao, w_co, w_mo, g_post)


def _halo_before(n):
    return pl.BlockSpec((HALO, n), lambda i: (jnp.maximum(i * (TM // HALO) - 1, 0), 0))


def _halo_after(n, n_tiles):
    return pl.BlockSpec((HALO, n), lambda i: (jnp.minimum((i + 1) * (TM // HALO), n_tiles * (TM // HALO) - 1), 0))


def _conv_fwd(z, z_before, cw):
    row = lax.broadcasted_iota(jnp.int32, z.shape, 0)
    z1 = jnp.where(row == 0, z_before[HALO - 1:HALO, :], pltpu.roll(z, 1, axis=0))
    z2 = jnp.where(row == 0, z_before[HALO - 2:HALO - 1, :],
                   jnp.where(row == 1, z_before[HALO - 1:HALO, :], pltpu.roll(z, 2, axis=0)))
    return cw[0:1, :] * z2 + cw[1:2, :] * z1 + cw[2:3, :] * z, z1, z2


def _ffn_fwd(x1, g1, g2, w_up, w_down):
    t = x1.shape[0]

    def body(x_ref, g1_ref, g2_ref, wu_ref, wd_ref, x2_ref, h2_ref, f_ref):
        xv = x_ref[...]
        h2 = _rms_fwd(xv, g1_ref[...]).astype(BF16)
        h2_ref[...] = h2
        ra = jnp.maximum(_dot(h2, wu_ref[...]), 0.0)
        f = _dot((ra * ra).astype(BF16), wd_ref[...])
        f_ref[...] = f
        x2_ref[...] = xv + _rms_fwd(f, g2_ref[...])

    outs = (_sds((t, D_MODEL), F32), _sds((t, D_MODEL), BF16), _sds((t, D_MODEL), F32))
    return pl.pallas_call(
        body, name="ffn_fwd", grid=(t // TM,),
        in_specs=[_rows(TM, D_MODEL), _whole((1, D_MODEL)), _whole((1, D_MODEL)), _whole((D_MODEL, D_FF)),
                  _whole((D_FF, D_MODEL))],
        out_specs=tuple(_rows(TM, D_MODEL) for _ in outs), out_shape=outs,
        compiler_params=_params(("parallel",)),
    )(x1, g1, g2, w_up, w_down)


def _loss_grad(y, target):
    t = y.shape[0]

    def body(y_ref, t_ref, dy_ref, loss_ref):
        @pl.when(pl.program_id(0) == 0)
        def _():
            loss_ref[...] = jnp.zeros_like(loss_ref)

        err = y_ref[...] - t_ref[...]
        dy_ref[...] = err * (1.0 / D_MODEL)
        loss_ref[...] += 0.5 * jnp.sum(jnp.mean(err * err, axis=-1, keepdims=True), axis=0, keepdims=True)

    return pl.pallas_call(
        body, name="loss_grad", grid=(t // TM,),
        in_specs=[_rows(TM, D_MODEL), _rows(TM, D_MODEL)],
        out_specs=(_rows(TM, D_MODEL), _whole((8, 128))),
        out_shape=(_sds((t, D_MODEL), F32), _sds((8, 128), F32)),
        compiler_params=_params(("arbitrary",)),
    )(y, target)


def _ffn_bwd(dx2, f, h2, x1, g1, g2, w_up, w_up_t, w_down_t):
    t = x1.shape[0]

    def body(dx2_ref, f_ref, h2_ref, x1_ref, g1_ref, g2_ref, wu_ref, wut_ref, wdt_ref, dx1_ref, da_ref, r_ref, df_ref,
             dg1_ref, dg2_ref):
        @pl.when(pl.program_id(0) == 0)
        def _():
            dg1_ref[...] = jnp.zeros_like(dg1_ref)
            dg2_ref[...] = jnp.zeros_like(dg2_ref)

        dx2 = dx2_ref[...]
        df, dg2 = _rms_bwd(f_ref[...], g2_ref[...], dx2)
        dg2_ref[...] += _colsum(dg2)
        dfb = df.astype(BF16)
        df_ref[...] = dfb
        ra = jnp.maximum(_dot(h2_ref[...], wu_ref[...]), 0.0)
        r_ref[...] = (ra * ra).astype(BF16)
        da = (_dot(dfb, wdt_ref[...]) * (2.0 * ra)).astype(BF16)
        da_ref[...] = da
        dxn, dg1 = _rms_bwd(x1_ref[...], g1_ref[...], _dot(da, wut_ref[...]))
        dg1_ref[...] += _colsum(dg1)
        dx1_ref[...] = dx2 + dxn

    outs = (_sds((t, D_MODEL), F32), _sds((t, D_FF), BF16), _sds((t, D_FF), BF16), _sds((t, D_MODEL), BF16),
            _sds((1, D_MODEL), F32), _sds((1, D_MODEL), F32))
    return pl.pallas_call(
        body, name="ffn_bwd", grid=(t // TM,),
        in_specs=[_rows(TM, D_MODEL), _rows(TM, D_MODEL), _rows(TM, D_MODEL), _rows(TM, D_MODEL), _whole((1, D_MODEL)),
                  _whole((1, D_MODEL)), _whole((D_MODEL, D_FF)), _whole((D_FF, D_MODEL)), _whole((D_MODEL, D_FF))],
        out_specs=(_rows(TM, D_MODEL), _rows(TM, D_FF), _rows(TM, D_FF), _rows(TM, D_MODEL), _whole((1, D_MODEL)),
                   _whole((1, D_MODEL))),
        out_shape=outs, compiler_params=_params(("arbitrary",), 60),
    )(dx2, f, h2, x1, g1, g2, w_up, w_up_t, w_down_t)


def _mm_tn(a, b):
    t, k = a.shape
    n = b.shape[1]
    tk, tt = min(k, 512), min(t, 512)
    tn = next(c for c in (1024, 512, 256) if n % c == 0)

    def body(a_ref, b_ref, o_ref):
        @pl.when(pl.program_id(2) == 0)
        def _():
            o_ref[...] = jnp.zeros_like(o_ref)

        o_ref[...] += _dot_tn(a_ref[...], b_ref[...])

    return pl.pallas_call(
        body, name="mm_tn", grid=(k // tk, n // tn, t // tt),
        in_specs=[pl.BlockSpec((tt, tk), lambda i, j, s: (s, i)), pl.BlockSpec((tt, tn), lambda i, j, s: (s, j))],
        out_specs=pl.BlockSpec((tk, tn), lambda i, j, s: (i, j)), out_shape=_sds((k, n), F32),
        compiler_params=_params(("parallel", "parallel", "arbitrary")),
    )(a, b)


def _merge_bwd(dx1, m2, g_post, w_mo_t, gate, y_pre, o, cb, cc, cx, conv_w, w_val, w_glu, w_ao, w_co, w_val_t, w_glu_t,
               w_ao_t, w_co_t):
    t = dx1.shape[0]

    def body(dx1_ref, m2_ref, gp_ref, wmot_ref, gate_ref, y_ref, o_ref, cb_ref, cc_ref, cx_ref, ccp_ref, cxp_ref, cw_ref,
             wv_ref, wg_ref, wao_ref, wco_ref, wvt_ref, wgt_ref, waot_ref, wcot_ref,
             dgp_ref, dyp_ref, do_ref, dcb_ref, dyc_ref, dm2_ref, da_ref, dbg_ref, yg_ref, dyb_ref, dycc_ref, cy_ref,
             dgpost_ref, dbgate_ref):
        i = pl.program_id(0)

        @pl.when(i == 0)
        def _():
            dgpost_ref[...] = jnp.zeros_like(dgpost_ref)
            dbgate_ref[...] = jnp.zeros_like(dbgate_ref)

        dm2, dgpost = _rms_bwd(m2_ref[...], gp_ref[...], dx1_ref[...])
        dgpost_ref[...] += _colsum(dgpost)
        dm2b = dm2.astype(BF16)
        dm2_ref[...] = dm2b
        dmerged = _dot(dm2b, wmot_ref[...])

        yv = y_ref[...]
        ygb = _gelu(yv).astype(BF16)
        yg_ref[...] = ygb
        a_val = _dot(ygb, wv_ref[...])
        s_glu = _sigmoid(_dot(ygb, wg_ref[...]))
        y_a = a_val * s_glu
        y_b = _dot(o_ref[...], wao_ref[...])
        cbv = cb_ref[...]
        yconv = _conv_fwd(cc_ref[...] * cx_ref[...], jnp.where(i > 0, ccp_ref[...] * cxp_ref[...], 0.0), cw_ref[...])[0]
        cyb = (cbv * yconv).astype(BF16)
        cy_ref[...] = cyb
        y_c = _dot(cyb, wco_ref[...])

        gate_v = gate_ref[...]
        g_a, g_b, g_c = gate_v[:, :D_MODEL], gate_v[:, D_MODEL:2 * D_MODEL], gate_v[:, 2 * D_MODEL:]
        dgp = jnp.concatenate([dmerged * y_a * g_a * (1.0 - g_a), dmerged * y_b * g_b * (1.0 - g_b),
                               dmerged * y_c * g_c * (1.0 - g_c)], axis=1)
        dbgate_ref[...] += _colsum(dgp)
        dgp_ref[...] = dgp.astype(BF16)

        dy_a = dmerged * g_a
        d_val = (dy_a * s_glu).astype(BF16)
        d_glu = (dy_a * a_val * s_glu * (1.0 - s_glu)).astype(BF16)
        da_ref[...] = d_val
        dbg_ref[...] = d_glu
        dyp_ref[...] = (_dot(d_val, wvt_ref[...]) + _dot(d_glu, wgt_ref[...])) * _gelu_grad(yv)

        dy_b = (dmerged * g_b).astype(BF16)
        dyb_ref[...] = dy_b
        do_ref[...] = _dot(dy_b, waot_ref[...]).astype(BF16)

        dy_c = (dmerged * g_c).astype(BF16)
        dycc_ref[...] = dy_c
        dcy = _dot(dy_c, wcot_ref[...])
        dcb_ref[...] = dcy * yconv
        dyc_ref[...] = dcy * cbv

    outs = (_sds((t, D_GATE), BF16), _sds((t, D_SSM), F32), _sds((t, D_SB), BF16), _sds((t, D_CONV), F32),
            _sds((t, D_CONV), F32), _sds((t, D_MODEL), BF16), _sds((t, D_MODEL), BF16), _sds((t, D_MODEL), BF16),
            _sds((t, D_SSM), BF16), _sds((t, D_MODEL), BF16), _sds((t, D_MODEL), BF16), _sds((t, D_CONV), BF16),
            _sds((1, D_MODEL), F32), _sds((1, D_GATE), F32))
    out_specs = tuple(_rows(TM, s.shape[1]) for s in outs[:-2]) + (_whole((1, D_MODEL)), _whole((1, D_GATE)))
    return pl.pallas_call(
        body, name="merge_bwd", grid=(t // TM,),
        in_specs=[_rows(TM, D_MODEL), _rows(TM, D_MODEL), _whole((1, D_MODEL)), _whole((D_MODEL, D_MODEL)),
                  _rows(TM, D_GATE), _rows(TM, D_SSM), _rows(TM, D_SB), _rows(TM, D_CONV), _rows(TM, D_CONV),
                  _rows(TM, D_CONV), _halo_before(D_CONV), _halo_before(D_CONV), _whole((3, D_CONV)),
                  _whole((D_SSM, D_MODEL)), _whole((D_SSM, D_MODEL)), _whole((D_SB, D_MODEL)),
                  _whole((D_CONV, D_MODEL)), _whole((D_MODEL, D_SSM)), _whole((D_MODEL, D_SSM)),
                  _whole((D_MODEL, D_SB)), _whole((D_MODEL, D_CONV))],
        out_specs=out_specs, out_shape=outs, compiler_params=_params(("arbitrary",)),
    )(dx1, m2, g_post, w_mo_t, gate, y_pre, o, cb, cc, cx, cc, cx, conv_w, w_val, w_glu, w_ao, w_co, w_val_t, w_glu_t,
      w_ao_t, w_co_t)


def _conv_bwd(dyconv, cc, cx, conv_w):
    t = dyconv.shape[0]
    nt = t // TM

    def body(dy_ref, dya_ref, cc_ref, cx_ref, ccp_ref, cxp_ref, cw_ref, dcc_ref, dcx_ref, dcw_ref):
        i = pl.program_id(0)

        @pl.when(i == 0)
        def _():
            dcw_ref[...] = jnp.zeros_like(dcw_ref)

        dy = dy_ref[...]
        dy_after = jnp.where(i < nt - 1, dya_ref[...], 0.0)
        row = lax.broadcasted_iota(jnp.int32, dy.shape, 0)
        dy1 = jnp.where(row == TM - 1, dy_after[0:1, :], pltpu.roll(dy, TM - 1, axis=0))
        dy2 = jnp.where(row == TM - 1, dy_after[1:2, :],
                        jnp.where(row == TM - 2, dy_after[0:1, :], pltpu.roll(dy, TM - 2, axis=0)))
        cw = cw_ref[...]
        dz = cw[2:3, :] * dy + cw[1:2, :] * dy1 + cw[0:1, :] * dy2
        ccv, cxv = cc_ref[...], cx_ref[...]
        dcc_ref[...] = dz * cxv
        dcx_ref[...] = dz * ccv
        z = ccv * cxv
        _, z1, z2 = _conv_fwd(z, jnp.where(i > 0, ccp_ref[...] * cxp_ref[...], 0.0), cw)
        dcw_ref[0:1, :] += _colsum(dy * z2)
        dcw_ref[1:2, :] += _colsum(dy * z1)
        dcw_ref[2:3, :] += _colsum(dy * z)

    return pl.pallas_call(
        body, name="conv_bwd", grid=(nt,),
        in_specs=[_rows(TM, D_CONV), _halo_after(D_CONV, nt), _rows(TM, D_CONV), _rows(TM, D_CONV),
                  _halo_before(D_CONV), _halo_before(D_CONV), _whole((3, D_CONV))],
        out_specs=(_rows(TM, D_CONV), _rows(TM, D_CONV), _whole((8, D_CONV))),
        out_shape=(_sds((t, D_CONV), F32), _sds((t, D_CONV), F32), _sds((8, D_CONV), F32)),
        compiler_params=_params(("arbitrary",)),
    )(dyconv, dyconv, cc, cx, cc, cx, conv_w)


def _attn_bwd(q_r, q_t, do_r, do_t, k_t, k_r, v_t, r_tab):
    h, t, _ = q_r.shape
    nk = t // BQ

    def body(q_ref, qt_ref, do_ref, dot_ref, kt_ref, kr_ref, vt_ref, rtab_ref, dq_ref, dkt_ref, dvt_ref):
        i = pl.program_id(1)

        @pl.when(i == 0)
        def _():
            dkt_ref[...] = jnp.zeros_like(dkt_ref)
            dvt_ref[...] = jnp.zeros_like(dvt_ref)

        q, q_tr, do, do_tr, rtab = q_ref[0], qt_ref[0, 0], do_ref[0], dot_ref[0, 0], rtab_ref[0]
        rows = lax.broadcasted_iota(jnp.int32, (BQ, BQ), 0)
        cols = lax.broadcasted_iota(jnp.int32, (BQ, BQ), 1)
        tri = rows > cols
        upper = tri.astype(BF16)
        lower = (rows < cols).astype(BF16)
        lane = lax.broadcasted_iota(jnp.int32, (BQ, R_LANES), 1)

        def tile(j, p_run, dq, diag):
            r_run = jnp.sum(jnp.where(lane == j, rtab, 0.0), axis=1, keepdims=True)
            a, _, _, w = _sb_tile(q, kt_ref[0, j], tri, upper, r_run, diag)
            e = w * _dot(do, vt_ref[0, j])
            e_hi = e.astype(BF16)
            e_lo = (e - e_hi.astype(F32)).astype(BF16)
            p = p_run + _dot(e_hi, lower) + _dot(e_lo, lower)
            beta = jnp.exp(a)
            dz = e * (1.0 - beta) - p * beta
            if diag:
                dz = jnp.where(tri, dz, 0.0)
            dzb = dz.astype(BF16)
            dvt_ref[0, j] += _dot(do_tr, w.astype(BF16))
            dkt_ref[0, j] += _dot(q_tr, dzb)
            return p_run + jnp.sum(e, axis=1, keepdims=True), dq + _dot(dzb, kr_ref[0, j])

        carry = lax.fori_loop(0, i, lambda j, c: tile(j, *c, False),
                              (jnp.zeros((BQ, 1), F32), jnp.zeros((BQ, SB_HEAD_DIM), F32)))
        dq_ref[0] = tile(i, *carry, True)[1] * SB_SCALE

    row_blk = pl.BlockSpec((1, BQ, SB_HEAD_DIM), lambda hh, i: (hh, i, 0))
    col_blk = pl.BlockSpec((1, 1, SB_HEAD_DIM, BQ), lambda hh, i: (hh, i, 0, 0))
    head_t = pl.BlockSpec((1, nk, SB_HEAD_DIM, BQ), lambda hh, i: (hh, 0, 0, 0))
    head_r = pl.BlockSpec((1, nk, BQ, SB_HEAD_DIM), lambda hh, i: (hh, 0, 0, 0))
    return pl.pallas_call(
        body, name="attn_bwd", grid=(h, nk),
        in_specs=[row_blk, col_blk, row_blk, col_blk, head_t, head_r, head_t,
                  pl.BlockSpec((1, BQ, R_LANES), lambda hh, i: (hh, i, 0))],
        out_specs=(row_blk, head_t, head_t),
        out_shape=(_sds((h, t, SB_HEAD_DIM), F32), _sds((h, nk, SB_HEAD_DIM, BQ), F32),
                   _sds((h, nk, SB_HEAD_DIM, BQ), F32)),
        compiler_params=_params(("parallel", "arbitrary")),
    )(q_r, q_t, do_r, do_t, k_t, k_r, v_t, r_tab)


def _ssm_bwd(dy, u, h_re, h_im, ct_re, ct_im, bt_re, bt_im, ab_re, ab_im, d_skip):
    t = u.shape[0]
    nt = t // TM

    def rev(n):
        return pl.BlockSpec((TM, n), lambda i: (nt - 1 - i, 0))

    def before(n):
        return pl.BlockSpec((HALO, n), lambda i: (jnp.maximum((nt - 1 - i) * (TM // HALO) - 1, 0), 0))

    def body(dy_ref, u_ref, hre_ref, him_ref, hrp_ref, hip_ref, ctre_ref, ctim_ref, btre_ref, btim_ref, ar_ref, ai_ref,
             d_ref, du_ref, dbre_ref, dbim_ref, dcre_ref, dcim_ref, dar_ref, dai_ref, dd_ref,
             ghr_ref, ghi_ref, lamr_ref, lami_ref, car_ref, cai_ref):
        i = pl.program_id(0)

        @pl.when(i == 0)
        def _():
            for ref in (car_ref, cai_ref, dbre_ref, dbim_ref, dcre_ref, dcim_ref, dar_ref, dai_ref, dd_ref):
                ref[...] = jnp.zeros_like(ref)

        dyv = dy_ref[...]
        dyb = dyv.astype(BF16)
        ghr_ref[...] = _dot(dyb, ctre_ref[...])
        ghi_ref[...] = -_dot(dyb, ctim_ref[...])
        _ssm_scan_rows(TM, ar_ref[...], ai_ref[...], -1.0, ghr_ref, ghi_ref, lamr_ref, lami_ref, car_ref, cai_ref, True)
        lam_r, lam_i = lamr_ref[...], lami_ref[...]
        lam_rb, lam_ib = lam_r.astype(BF16), lam_i.astype(BF16)
        uv = u_ref[...]
        ub = uv.astype(BF16)
        du_ref[...] = _dot(lam_rb, btre_ref[...]) + _dot(lam_ib, btim_ref[...]) + d_ref[...] * dyv
        dbre_ref[...] += _dot_tn(ub, lam_rb)
        dbim_ref[...] += _dot_tn(ub, lam_ib)
        h_r, h_i = hre_ref[...], him_ref[...]
        dcre_ref[...] += _dot_tn(dyb, h_r.astype(BF16))
        dcim_ref[...] -= _dot_tn(dyb, h_i.astype(BF16))
        first = i == nt - 1
        row = lax.broadcasted_iota(jnp.int32, h_r.shape, 0)
        h_r1 = jnp.where(row == 0, jnp.where(first, 0.0, hrp_ref[HALO - 1:HALO, :]), pltpu.roll(h_r, 1, axis=0))
        h_i1 = jnp.where(row == 0, jnp.where(first, 0.0, hip_ref[HALO - 1:HALO, :]), pltpu.roll(h_i, 1, axis=0))
        dar_ref[...] += _colsum(lam_r * h_r1 + lam_i * h_i1)
        dai_ref[...] += _colsum(lam_i * h_r1 - lam_r * h_i1)
        dd_ref[...] += _colsum(dyv * uv)

    outs = (_sds((t, D_SSM), F32), _sds((D_SSM, N_STATE), F32), _sds((D_SSM, N_STATE), F32),
            _sds((D_SSM, N_STATE), F32), _sds((D_SSM, N_STATE), F32), _sds((1, N_STATE), F32),
            _sds((1, N_STATE), F32), _sds((1, D_SSM), F32))
    return pl.pallas_call(
        body, name="ssm_bwd", grid=(nt,),
        in_specs=[rev(D_SSM), rev(D_SSM), rev(N_STATE), rev(N_STATE), before(N_STATE), before(N_STATE),
                  _whole((D_SSM, N_STATE)), _whole((D_SSM, N_STATE)), _whole((N_STATE, D_SSM)),
                  _whole((N_STATE, D_SSM)), _whole((1, N_STATE)), _whole((1, N_STATE)), _whole((1, D_SSM))],
        out_specs=(rev(D_SSM),) + tuple(_whole(s.shape) for s in outs[1:]), out_shape=outs,
        scratch_shapes=[pltpu.VMEM((TM, N_STATE), F32) for _ in range(4)]
        + [pltpu.VMEM((1, N_STATE), F32), pltpu.VMEM((1, N_STATE), F32)],
        compiler_params=_params(("arbitrary",)),
    )(dy, u, h_re, h_im, h_re, h_im, ct_re, ct_im, bt_re, bt_im, ab_re, ab_im, d_skip)


def _in_bwd(dp, dgp, w_in_t, w_gate_t, x, g_pre, dx_res):
    t = x.shape[0]

    def body(dp_ref, dgp_ref, wit_ref, wgt_ref, x_ref, g_ref, dxr_ref, dx_ref, dg_ref):
        @pl.when(pl.program_id(0) == 0)
        def _():
            dg_ref[...] = jnp.zeros_like(dg_ref)

        dh = _dot(dp_ref[...], wit_ref[...]) + _dot(dgp_ref[...], wgt_ref[...])
        dxn, dg = _rms_bwd(x_ref[...], g_ref[...], dh)
        dg_ref[...] += _colsum(dg)
        dx_ref[...] = dxr_ref[...] + dxn

    return pl.pallas_call(
        body, name="in_bwd", grid=(t // TM,),
        in_specs=[_rows(TM, D_IN), _rows(TM, D_GATE), _whole((D_IN, D_MODEL)), _whole((D_GATE, D_MODEL)),
                  _rows(TM, D_MODEL), _whole((1, D_MODEL)), _rows(TM, D_MODEL)],
        out_specs=(_rows(TM, D_MODEL), _whole((1, D_MODEL))),
        out_shape=(_sds((t, D_MODEL), F32), _sds((1, D_MODEL), F32)),
        compiler_params=_params(("arbitrary",)),
    )(dp, dgp, w_in_t, w_gate_t, x, g_pre, dx_res)


def _ssm_disc(lr, li, ldt):
    dt = jnp.exp(ldt)
    mag = jnp.exp(lr * dt)
    th = li * dt
    cs, sn = jnp.cos(th), jnp.sin(th)
    ab_re, ab_im = mag * cs, mag * sn
    den = lr * lr + li * li
    xr = ab_re - 1.0
    co_re = (xr * lr + ab_im * li) / den
    co_im = (ab_im * lr - xr * li) / den
    return dt, mag, cs, sn, ab_re, ab_im, den, xr, co_re, co_im


def _ssm_param_fwd(lr, li, ldt, b_re_t, b_im_t):
    def body(lr_ref, li_ref, ldt_ref, br_ref, bi_ref, are_ref, aim_ref, bbr_ref, bbi_ref):
        _, _, _, _, ab_re, ab_im, _, _, co_re, co_im = _ssm_disc(lr_ref[...], li_ref[...], ldt_ref[...])
        are_ref[...] = ab_re
        aim_ref[...] = ab_im
        br, bi = br_ref[...], bi_ref[...]
        bbr_ref[...] = co_re * br - co_im * bi
        bbi_ref[...] = co_re * bi + co_im * br

    vec, mat = _sds((1, N_STATE), F32), _sds((SSM_GROUP, N_STATE), F32)
    return pl.pallas_call(body, name="ssm_param_fwd", out_shape=(vec, vec, mat, mat))(lr, li, ldt, b_re_t, b_im_t)


def _ssm_param_bwd(lr, li, ldt, b_re_t, b_im_t, dab_re, dab_im, dbb_re_t, dbb_im_t):
    def body(lr_ref, li_ref, ldt_ref, br_ref, bi_ref, dar_ref, dai_ref, dbbr_ref, dbbi_ref,
             dlr_ref, dli_ref, dldt_ref, dbr_ref, dbi_ref):
        lr, li = lr_ref[...], li_ref[...]
        dt, mag, cs, sn, ab_re, ab_im, den, xr, co_re, co_im = _ssm_disc(lr, li, ldt_ref[...])
        br, bi, dbbr, dbbi = br_ref[...], bi_ref[...], dbbr_ref[...], dbbi_ref[...]
        dbr_ref[...] = co_re * dbbr + co_im * dbbi
        dbi_ref[...] = co_re * dbbi - co_im * dbbr
        dco_re = _colsum(br * dbbr + bi * dbbi)
        dco_im = _colsum(br * dbbi - bi * dbbr)
        dxr = (dco_re * lr - dco_im * li) / den
        dab_i = dai_ref[...] + (dco_re * li + dco_im * lr) / den
        dab_r = dar_ref[...] + dxr
        dden = -(co_re * dco_re + co_im * dco_im) / den
        dlr = (dco_re * xr + dco_im * ab_im) / den + dden * 2.0 * lr
        dli = (dco_re * ab_im - dco_im * xr) / den + dden * 2.0 * li
        dmag = dab_r * cs + dab_i * sn
        dth = mag * (dab_i * cs - dab_r * sn)
        dlr_ref[...] = dlr + dmag * mag * dt
        dli_ref[...] = dli + dth * dt
        dldt = jnp.broadcast_to((dmag * mag * lr + dth * li) * dt, (8, N_STATE))
        group = (lax.broadcasted_iota(jnp.int32, (N_STATE, 128), 0) // SSM_STATE
                 == lax.broadcasted_iota(jnp.int32, (N_STATE, 128), 1)).astype(F32)
        dldt_ref[...] = jnp.dot(dldt, group, precision=lax.Precision.HIGHEST, preferred_element_type=F32)

    vec, mat = _sds((1, N_STATE), F32), _sds((SSM_GROUP, N_STATE), F32)
    return pl.pallas_call(body, name="ssm_param_bwd", out_shape=(vec, vec, _sds((8, 128), F32), mat, mat))(
        lr, li, ldt, b_re_t, b_im_t, dab_re, dab_im, dbb_re_t, dbb_im_t)


ANY = pl.BlockSpec(memory_space=pl.ANY)


def _chip_peers(x, y):
    return [(1 - x, y), (x, 1 - y), (1 - x, 1 - y)]


def _gather_chips(buf):
    def body(src_ref, out_ref, send_sems, recv_sems, local_sem):
        x, y, c = lax.axis_index("x"), lax.axis_index("y"), lax.axis_index("c")
        local = pltpu.make_async_copy(src_ref, out_ref.at[2 * x + y], local_sem)
        local.start()
        sends = []
        for k, (px, py) in enumerate(_chip_peers(x, y)):
            sends.append(pltpu.make_async_remote_copy(src_ref, out_ref.at[2 * x + y], send_sems.at[k], recv_sems.at[k],
                                                      device_id=(px, py, c), device_id_type=MESH))
            sends[-1].start()
        for k, (px, py) in enumerate(_chip_peers(x, y)):
            pltpu.make_async_remote_copy(src_ref, out_ref.at[2 * px + py], send_sems.at[k], recv_sems.at[k],
                                         device_id=(px, py, c), device_id_type=MESH).wait_recv()
        for cp in sends:
            cp.wait_send()
        local.wait()

    return pl.pallas_call(
        body, name="gather_chips", in_specs=[ANY], out_specs=ANY, out_shape=_sds((N_CHIPS,) + buf.shape, buf.dtype),
        scratch_shapes=[pltpu.SemaphoreType.DMA((3,)), pltpu.SemaphoreType.DMA((3,)), pltpu.SemaphoreType.DMA],
    )(buf)


def _scatter_chips(buf):
    def body(src_ref, out_ref, send_sems, recv_sems, local_sem):
        x, y, c = lax.axis_index("x"), lax.axis_index("y"), lax.axis_index("c")
        me = 2 * x + y
        local = pltpu.make_async_copy(src_ref.at[me], out_ref.at[me], local_sem)
        local.start()
        sends = []
        for k, (px, py) in enumerate(_chip_peers(x, y)):
            sends.append(pltpu.make_async_remote_copy(src_ref.at[2 * px + py], out_ref.at[me], send_sems.at[k],
                                                      recv_sems.at[k], device_id=(px, py, c), device_id_type=MESH))
            sends[-1].start()
        for k, (px, py) in enumerate(_chip_peers(x, y)):
            pltpu.make_async_remote_copy(src_ref.at[me], out_ref.at[2 * px + py], send_sems.at[k], recv_sems.at[k],
                                         device_id=(px, py, c), device_id_type=MESH).wait_recv()
        for cp in sends:
            cp.wait_send()
        local.wait()

    return pl.pallas_call(
        body, name="scatter_chips", in_specs=[ANY], out_specs=ANY, out_shape=_sds(buf.shape, buf.dtype),
        scratch_shapes=[pltpu.SemaphoreType.DMA((3,)), pltpu.SemaphoreType.DMA((3,)), pltpu.SemaphoreType.DMA],
    )(buf)


def _swap_cores(buf):
    def body(src_ref, out_ref, send_sem, recv_sem, local_sem):
        x, y, c = lax.axis_index("x"), lax.axis_index("y"), lax.axis_index("c")
        local = pltpu.make_async_copy(src_ref, out_ref.at[c], local_sem)
        local.start()
        send = pltpu.make_async_remote_copy(src_ref, out_ref.at[c], send_sem, recv_sem, device_id=(x, y, 1 - c),
                                            device_id_type=MESH)
        send.start()
        pltpu.make_async_remote_copy(src_ref, out_ref.at[1 - c], send_sem, recv_sem, device_id=(x, y, 1 - c),
                                     device_id_type=MESH).wait_recv()
        send.wait_send()
        local.wait()

    return pl.pallas_call(
        body, name="swap_cores", in_specs=[ANY], out_specs=ANY, out_shape=_sds((2,) + buf.shape, buf.dtype),
        scratch_shapes=[pltpu.SemaphoreType.DMA, pltpu.SemaphoreType.DMA, pltpu.SemaphoreType.DMA],
    )(buf)


def _gather_all(buf):
    flips = [(dx, dy, dc) for dx in (0, 1) for dy in (0, 1) for dc in (0, 1)][1:]

    def body(src_ref, out_ref, send_sems, recv_sems, local_sem):
        x, y, c = lax.axis_index("x"), lax.axis_index("y"), lax.axis_index("c")
        me = 4 * x + 2 * y + c
        peers = [(1 - x if dx else x, 1 - y if dy else y, 1 - c if dc else c) for dx, dy, dc in flips]
        local = pltpu.make_async_copy(src_ref, out_ref.at[me], local_sem)
        local.start()
        sends = []
        for k, peer in enumerate(peers):
            sends.append(pltpu.make_async_remote_copy(src_ref, out_ref.at[me], send_sems.at[k], recv_sems.at[k],
                                                      device_id=peer, device_id_type=MESH))
            sends[-1].start()
        for k, (px, py, pc) in enumerate(peers):
            pltpu.make_async_remote_copy(src_ref, out_ref.at[4 * px + 2 * py + pc], send_sems.at[k], recv_sems.at[k],
                                         device_id=(px, py, pc), device_id_type=MESH).wait_recv()
        for cp in sends:
            cp.wait_send()
        local.wait()

    return pl.pallas_call(
        body, name="gather_all", in_specs=[ANY], out_specs=ANY, out_shape=_sds((N_DEV,) + buf.shape, buf.dtype),
        scratch_shapes=[pltpu.SemaphoreType.DMA((7,)), pltpu.SemaphoreType.DMA((7,)), pltpu.SemaphoreType.DMA],
    )(buf)


PACK_W = 1024
PACK_ROWS = 256


def _sum_parts(parts):
    n, r, w = parts.shape

    def body(p_ref, o_ref):
        acc = p_ref[0]
        for k in range(1, n):
            acc = acc + p_ref[k]
        o_ref[...] = acc

    return pl.pallas_call(
        body, name="sum_parts", grid=(r // PACK_ROWS,),
        in_specs=[pl.BlockSpec((n, PACK_ROWS, w), lambda i: (0, i, 0))], out_specs=_rows(PACK_ROWS, w),
        out_shape=_sds((r, w), F32), compiler_params=_params(("parallel",)),
    )(parts)


def _adamw(parts, w, m, v):
    n, r, wd = parts.shape

    def body(p_ref, w_ref, m_ref, v_ref, g_ref, dw_ref, nm_ref, nv_ref):
        g = p_ref[0]
        for k in range(1, n):
            g = g + p_ref[k]
        g_ref[...] = g
        m_new = ADAM_B1 * m_ref[...] + (1.0 - ADAM_B1) * g
        v_new = ADAM_B2 * v_ref[...] + (1.0 - ADAM_B2) * (g * g)
        nm_ref[...] = m_new
        nv_ref[...] = v_new
        m_hat = m_new / (1.0 - ADAM_B1 ** ADAM_STEP)
        v_hat = v_new / (1.0 - ADAM_B2 ** ADAM_STEP)
        dw_ref[...] = -ADAM_LR * (m_hat / (jnp.sqrt(v_hat) + ADAM_EPS) + ADAM_WD * w_ref[...])

    blk = _rows(PACK_ROWS, wd)
    return pl.pallas_call(
        body, name="adamw", grid=(r // PACK_ROWS,),
        in_specs=[pl.BlockSpec((n, PACK_ROWS, wd), lambda i: (0, i, 0)), blk, blk, blk], out_specs=(blk,) * 4,
        out_shape=(_sds((r, wd), F32),) * 4, compiler_params=_params(("parallel",)),
    )(parts, w, m, v)


def _pack(arrs, dtype):
    unit = PACK_W * (8 if dtype == F32 else 16)
    flat = []
    for a in arrs:
        a = a.reshape(-1).astype(dtype)
        flat.append(jnp.pad(a, (0, -a.size % unit)))
    total = sum(f.size for f in flat)
    flat.append(jnp.zeros((-total % (PACK_W * PACK_ROWS),), dtype))
    return jnp.concatenate(flat).reshape(-1, PACK_W)


def _unpack(buf, shapes, dtype):
    unit = PACK_W * (8 if dtype == F32 else 16)
    flat = buf.reshape(-1)
    out, off = [], 0
    for shp in shapes:
        size = 1
        for s in shp:
            size *= s
        out.append(flat[off:off + size].reshape(shp))
        off += size + (-size % unit)
    return out


def _expand_bd(m_t):
    g_row = jnp.arange(D_SSM)[:, None] // SSM_GROUP
    g_col = jnp.arange(N_STATE)[None, :] // SSM_STATE
    return jnp.where(g_row == g_col, jnp.tile(m_t, (SSM_GROUPS, 1)), 0.0)


def _extract_bd(full):
    g_row = jnp.arange(D_SSM)[:, None] // SSM_GROUP
    g_col = jnp.arange(N_STATE)[None, :] // SSM_STATE
    return jnp.where(g_row == g_col, full, 0.0).reshape(SSM_GROUPS, SSM_GROUP, N_STATE).sum(0)


def _heads_rows(a):
    return a.reshape(a.shape[0], SB_HEADS, SB_HEAD_DIM).transpose(1, 0, 2)


def _heads_blocks_t(a):
    return a.reshape(a.shape[0] // BQ, BQ, SB_HEADS, SB_HEAD_DIM).transpose(2, 0, 3, 1)


def _heads_blocks_r(a):
    return a.reshape(a.shape[0] // BQ, BQ, SB_HEADS, SB_HEAD_DIM).transpose(2, 0, 1, 3)


def _prep_layer(w, l):
    p = {}
    for name in ('norm_mix_pre', 'norm_mix_post', 'b_gate', 'norm_ffn_pre', 'norm_ffn_post', 'ssm_d'):
        p[name] = w[name][l][None, :]
    for name in ('w_in', 'w_gate', 'w_glu_val', 'w_glu_gate', 'w_attn_out', 'w_conv_out', 'w_mix_out', 'w_ffn_up',
                 'w_ffn_down'):
        p[name] = w[name][l]
        p[name + '_t'] = w[name][l].T
    p['conv_w'] = w['conv_w'][l]
    p['lr'] = w['ssm_a_re'][l].reshape(1, N_STATE)
    p['li'] = w['ssm_a_im'][l].reshape(1, N_STATE)
    p['ldt'] = jnp.repeat(w['ssm_log_dt'][l], SSM_STATE).reshape(1, N_STATE)
    p['b_re_t'] = w['ssm_b_re'][l].transpose(2, 0, 1).reshape(SSM_GROUP, N_STATE)
    p['b_im_t'] = w['ssm_b_im'][l].transpose(2, 0, 1).reshape(SSM_GROUP, N_STATE)
    p['ab_re'], p['ab_im'], bb_re_t, bb_im_t = _ssm_param_fwd(p['lr'], p['li'], p['ldt'], p['b_re_t'], p['b_im_t'])
    p['bb_re'] = _expand_bd(bb_re_t).astype(BF16)
    p['bb_im'] = _expand_bd(bb_im_t).astype(BF16)
    p['ct_re'] = _expand_bd(w['ssm_c_re'][l].transpose(1, 0, 2).reshape(SSM_GROUP, N_STATE)).astype(BF16)
    p['ct_im'] = _expand_bd(w['ssm_c_im'][l].transpose(1, 0, 2).reshape(SSM_GROUP, N_STATE)).astype(BF16)
    return p


def _layer_fwd(x, p):
    s = {'x': x}
    s['hb'], s['u'], q, k, v, s['cb'], s['cc'], s['cx'], s['gate'] = _fwd_in(
        x, p['norm_mix_pre'], p['w_in'], p['w_gate'], p['b_gate'])
    s['h_re'], s['h_im'], s['y_pre'] = _ssm_fwd(s['u'], p['bb_re'], p['bb_im'], p['ab_re'], p['ab_im'], p['ct_re'].T,
                                                p['ct_im'].T, p['ssm_d'])
    s['q'], s['k'], s['v'] = q, k, v
    o_r, s['r_tab'] = _attn_fwd(_heads_rows(q), _heads_blocks_t(k), _heads_blocks_r(v))
    s['o'] = o_r.transpose(1, 0, 2).reshape(x.shape[0], D_SB).astype(BF16)
    s['x1'], s['merged'], s['m2'] = _merge_fwd(
        s['y_pre'], s['o'], s['cb'], s['cc'], s['cx'], p['conv_w'], s['gate'], x, p['w_glu_val'], p['w_glu_gate'],
        p['w_attn_out'], p['w_conv_out'], p['w_mix_out'], p['norm_mix_post'])
    x2, s['h2'], s['f'] = _ffn_fwd(s['x1'], p['norm_ffn_pre'], p['norm_ffn_post'], p['w_ffn_up'], p['w_ffn_down'])
    return x2, s


def _layer_bwd(dx2, p, s):
    t = dx2.shape[0]
    g = {}
    dx1, da, r, df, g['norm_ffn_pre'], g['norm_ffn_post'] = _ffn_bwd(
        dx2, s['f'], s['h2'], s['x1'], p['norm_ffn_pre'], p['norm_ffn_post'], p['w_ffn_up'], p['w_ffn_up_t'],
        p['w_ffn_down_t'])
    g['w_ffn_up'] = _mm_tn(s['h2'], da)
    g['w_ffn_down'] = _mm_tn(r, df)

    (dgp, dy_pre, do, dcb, dyconv, dm2, d_val, d_glu, yg, dy_b, dy_c, cy, g['norm_mix_post'], g['b_gate']) = _merge_bwd(
        dx1, s['m2'], p['norm_mix_post'], p['w_mix_out_t'], s['gate'], s['y_pre'], s['o'], s['cb'], s['cc'], s['cx'],
        p['conv_w'], p['w_glu_val'], p['w_glu_gate'], p['w_attn_out'], p['w_conv_out'], p['w_glu_val_t'],
        p['w_glu_gate_t'], p['w_attn_out_t'], p['w_conv_out_t'])
    g['w_mix_out'] = _mm_tn(s['merged'], dm2)
    g['w_glu_val'] = _mm_tn(yg, d_val)
    g['w_glu_gate'] = _mm_tn(yg, d_glu)
    g['w_attn_out'] = _mm_tn(s['o'], dy_b)
    g['w_conv_out'] = _mm_tn(cy, dy_c)

    dcc, dcx, dcw = _conv_bwd(dyconv, s['cc'], s['cx'], p['conv_w'])
    g['conv_w'] = dcw[0:3]

    dq_r, dk_t, dv_t = _attn_bwd(_heads_rows(s['q']), _heads_blocks_t(s['q']), _heads_rows(do), _heads_blocks_t(do),
                                 _heads_blocks_t(s['k']), _heads_blocks_r(s['k']), _heads_blocks_t(s['v']), s['r_tab'])
    dq = dq_r.transpose(1, 0, 2).reshape(t, D_SB)
    dk = dk_t.transpose(1, 3, 0, 2).reshape(t, D_SB)
    dv = dv_t.transpose(1, 3, 0, 2).reshape(t, D_SB)

    du, dbb_re, dbb_im, dct_re, dct_im, dab_re, dab_im, g['ssm_d'] = _ssm_bwd(
        dy_pre, s['u'], s['h_re'], s['h_im'], p['ct_re'], p['ct_im'], p['bb_re'].T, p['bb_im'].T, p['ab_re'],
        p['ab_im'], p['ssm_d'])
    dlr, dli, dldt, db_re_t, db_im_t = _ssm_param_bwd(p['lr'], p['li'], p['ldt'], p['b_re_t'], p['b_im_t'], dab_re,
                                                      dab_im, _extract_bd(dbb_re), _extract_bd(dbb_im))
    g['ssm_a_re'] = dlr.reshape(SSM_GROUPS, SSM_STATE)
    g['ssm_a_im'] = dli.reshape(SSM_GROUPS, SSM_STATE)
    g['ssm_log_dt'] = dldt[0, :SSM_GROUPS]
    g['ssm_b_re'] = db_re_t.reshape(SSM_GROUP, SSM_GROUPS, SSM_STATE).transpose(1, 2, 0)
    g['ssm_b_im'] = db_im_t.reshape(SSM_GROUP, SSM_GROUPS, SSM_STATE).transpose(1, 2, 0)
    g['ssm_c_re'] = _extract_bd(dct_re).reshape(SSM_GROUP, SSM_GROUPS, SSM_STATE).transpose(1, 0, 2)
    g['ssm_c_im'] = _extract_bd(dct_im).reshape(SSM_GROUP, SSM_GROUPS, SSM_STATE).transpose(1, 0, 2)

    dp = jnp.concatenate([du.astype(BF16), dq.astype(BF16), dk.astype(BF16), dv.astype(BF16), dcb.astype(BF16),
                          dcc.astype(BF16), dcx.astype(BF16)], axis=1)
    dx, g['norm_mix_pre'] = _in_bwd(dp, dgp, p['w_in_t'], p['w_gate_t'], s['x'], p['norm_mix_pre'], dx1)
    g['w_in'] = _mm_tn(s['hb'], dp)
    g['w_gate'] = _mm_tn(s['hb'], dgp)
    for name in ('norm_mix_pre', 'norm_mix_post', 'b_gate', 'norm_ffn_pre', 'norm_ffn_post', 'ssm_d'):
        g[name] = g[name][0]
    return dx, g


def _local_step(x, target, w):
    ps, saved = [], []
    for l in range(DEPTH):
        ps.append(_prep_layer(w, l))
        x, s = _layer_fwd(x, ps[l])
        saved.append(s)
    dx, loss_blk = _loss_grad(x, target)
    grads = [None] * DEPTH
    for l in reversed(range(DEPTH)):
        dx, grads[l] = _layer_bwd(dx, ps[l], saved[l])
    return loss_blk, dx, {n: jnp.stack([grads[l][n] for l in range(DEPTH)]) for n in WEIGHTS}


def kernel(x, norm_mix_pre, norm_mix_post, w_in, w_gate, b_gate, ssm_a_re, ssm_a_im, ssm_log_dt, ssm_b_re,
           ssm_b_im, ssm_c_re, ssm_c_im, ssm_d, w_glu_val, w_glu_gate, w_attn_out, conv_w, w_conv_out,
           w_mix_out, norm_ffn_pre, norm_ffn_post, w_ffn_up, w_ffn_down, loss_target, m_norm_mix_pre,
           m_norm_mix_post, m_w_in, m_w_gate, m_b_gate, m_ssm_a_re, m_ssm_a_im, m_ssm_log_dt, m_ssm_b_re,
           m_ssm_b_im, m_ssm_c_re, m_ssm_c_im, m_ssm_d, m_w_glu_val, m_w_glu_gate, m_w_attn_out, m_conv_w,
           m_w_conv_out, m_w_mix_out, m_norm_ffn_pre, m_norm_ffn_post, m_w_ffn_up, m_w_ffn_down,
           v_norm_mix_pre, v_norm_mix_post, v_w_in, v_w_gate, v_b_gate, v_ssm_a_re, v_ssm_a_im, v_ssm_log_dt,
           v_ssm_b_re, v_ssm_b_im, v_ssm_c_re, v_ssm_c_im, v_ssm_d, v_w_glu_val, v_w_glu_gate, v_w_attn_out,
           v_conv_w, v_w_conv_out, v_w_mix_out, v_norm_ffn_pre, v_norm_ffn_post, v_w_ffn_up, v_w_ffn_down):
    given = dict(locals())
    shard = {n: given[n] for n in WEIGHTS}
    big = list(SHARDED)

    sent = [lax.bitcast_convert_type(shard[n], BF16) if n == 'conv_w' else shard[n].astype(BF16) for n in big]
    gathered = _gather_chips(_pack(sent, BF16))
    full = {n: shard[n] for n in REPLICATED}
    per_chip = [_unpack(gathered[j], [a.shape for a in sent], BF16) for j in range(N_CHIPS)]
    for i, n in enumerate(big):
        parts = [per_chip[j][i] for j in range(N_CHIPS)]
        if n == 'conv_w':
            parts = [lax.bitcast_convert_type(a, F32) for a in parts]
        full[n] = jnp.concatenate(parts, axis=SHARDED[n])

    loss_blk, dx, grads = _local_step(x[0], loss_target[0], full)
    loss = lax.psum(loss_blk[0, 0], ("x", "y", "c"))

    to_chip = [_pack([jnp.split(grads[n], N_CHIPS, axis=SHARDED[n])[j] for n in big], F32) for j in range(N_CHIPS)]
    core_sum = _sum_parts(_scatter_chips(jnp.stack(to_chip)))
    out_big = _adamw(_swap_cores(core_sum), _pack([shard[n] for n in big], F32),
                     _pack([given['m_' + n] for n in big], F32), _pack([given['v_' + n] for n in big], F32))
    out_small = _adamw(_gather_all(_pack([grads[n] for n in REPLICATED], F32)),
                       _pack([shard[n] for n in REPLICATED], F32), _pack([given['m_' + n] for n in REPLICATED], F32),
                       _pack([given['v_' + n] for n in REPLICATED], F32))

    results = []
    for kind in range(4):
        got = dict(zip(big, _unpack(out_big[kind], [shard[n].shape for n in big], F32)))
        got.update(zip(REPLICATED, _unpack(out_small[kind], [shard[n].shape for n in REPLICATED], F32)))
        results += [got[n] for n in WEIGHTS]
    return (loss, dx[None], *results)
```

```python
import functools

import jax
import jax.numpy as jnp
from jax import lax
from jax.experimental import pallas as pl
from jax.experimental.pallas import tpu as pltpu

F32, BF16 = jnp.float32, jnp.bfloat16

D_MODEL = 1024
DEPTH = 2
SSM_GROUPS, SSM_GROUP, SSM_STATE = 16, 16, 64
D_SSM = SSM_GROUPS * SSM_GROUP
N_STATE = SSM_GROUPS * SSM_STATE
SB_HEADS, SB_HEAD_DIM = 8, 64
D_SB = SB_HEADS * SB_HEAD_DIM
D_CONV = 256
D_IN = D_SSM + 3 * D_SB + 3 * D_CONV
D_GATE = 3 * D_MODEL
D_FF = 4 * D_MODEL
EPS = 1e-6
SB_SCALE = SB_HEAD_DIM ** -0.5
GELU_C = 0.7978845608028654
GELU_A = 0.044715

ADAM_LR, ADAM_B1, ADAM_B2, ADAM_EPS, ADAM_WD, ADAM_STEP = 0.001, 0.9, 0.999, 1e-08, 0.01, 10

TM = 256
BQ = 256
SB_UNROLL = 4
R_LANES = 128
HALO = 8
VMEM_LIMIT_MB = 56

MESH = pl.DeviceIdType.MESH
WEIGHTS = ['norm_mix_pre', 'norm_mix_post', 'w_in', 'w_gate', 'b_gate', 'ssm_a_re', 'ssm_a_im', 'ssm_log_dt',
           'ssm_b_re', 'ssm_b_im', 'ssm_c_re', 'ssm_c_im', 'ssm_d', 'w_glu_val', 'w_glu_gate', 'w_attn_out', 'conv_w',
           'w_conv_out', 'w_mix_out', 'norm_ffn_pre', 'norm_ffn_post', 'w_ffn_up', 'w_ffn_down']
SHARDED = {'w_in': 2, 'w_gate': 2, 'w_glu_val': 2, 'w_glu_gate': 2, 'w_attn_out': 2, 'conv_w': 2, 'w_conv_out': 2,
           'w_mix_out': 1, 'w_ffn_up': 2, 'w_ffn_down': 1}
REPLICATED = [n for n in WEIGHTS if n not in SHARDED]
N_CHIPS = 4
N_DEV = 8


def _dot(a, b):
    return jnp.dot(a, b, preferred_element_type=F32)


def _dot_tn(a, b):
    return lax.dot_general(a, b, (((0,), (0,)), ((), ())), preferred_element_type=F32)


def _sigmoid(x):
    return 1.0 / (1.0 + jnp.exp(-x))


def _params(sem, vmem_mb=VMEM_LIMIT_MB):
    return pltpu.CompilerParams(dimension_semantics=sem, vmem_limit_bytes=vmem_mb << 20)


def _rows(tm, n):
    return pl.BlockSpec((tm, n), lambda i: (i, 0))


def _whole(shape):
    zeros = (0,) * len(shape)
    return pl.BlockSpec(shape, lambda *_: zeros)


def _sds(shape, dtype):
    return jax.ShapeDtypeStruct(shape, dtype)


def _rms_fwd(x, g):
    r = lax.rsqrt(jnp.mean(x * x, axis=-1, keepdims=True) + EPS)
    return x * r * g


def _rms_bwd(x, g, dy):
    r = lax.rsqrt(jnp.mean(x * x, axis=-1, keepdims=True) + EPS)
    xh = x * r
    dxh = dy * g
    dx = r * (dxh - xh * jnp.mean(dxh * xh, axis=-1, keepdims=True))
    return dx, dy * xh


def _colsum(a):
    return jnp.sum(a, axis=0, keepdims=True)


def _gelu(y):
    return 0.5 * y * (1.0 + jnp.tanh(GELU_C * (y + GELU_A * y * y * y)))


def _gelu_grad(y):
    th = jnp.tanh(GELU_C * (y + GELU_A * y * y * y))
    return 0.5 * (1.0 + th) + 0.5 * y * (1.0 - th * th) * GELU_C * (1.0 + 3.0 * GELU_A * y * y)


def _fwd_in(x, g_pre, w_in, w_gate, b_gate):
    t = x.shape[0]

    def body(x_ref, g_ref, win_ref, wg_ref, bg_ref, hb_ref, u_ref, q_ref, k_ref, v_ref, cb_ref, cc_ref, cx_ref, gate_ref):
        hb = _rms_fwd(x_ref[...], g_ref[...]).astype(BF16)
        hb_ref[...] = hb
        p = _dot(hb, win_ref[...])
        o = 0
        u_ref[...] = p[:, o:o + D_SSM]
        o += D_SSM
        q_ref[...] = (p[:, o:o + D_SB] * -SB_SCALE).astype(BF16)
        o += D_SB
        k_ref[...] = p[:, o:o + D_SB].astype(BF16)
        o += D_SB
        v_ref[...] = p[:, o:o + D_SB].astype(BF16)
        o += D_SB
        cb_ref[...] = p[:, o:o + D_CONV]
        o += D_CONV
        cc_ref[...] = p[:, o:o + D_CONV]
        o += D_CONV
        cx_ref[...] = p[:, o:o + D_CONV]
        gate_ref[...] = _sigmoid(_dot(hb, wg_ref[...]) + bg_ref[...])

    outs = (_sds((t, D_MODEL), BF16), _sds((t, D_SSM), F32), _sds((t, D_SB), BF16), _sds((t, D_SB), BF16),
            _sds((t, D_SB), BF16), _sds((t, D_CONV), F32), _sds((t, D_CONV), F32), _sds((t, D_CONV), F32),
            _sds((t, D_GATE), F32))
    return pl.pallas_call(
        body, name="fwd_in", grid=(t // TM,),
        in_specs=[_rows(TM, D_MODEL), _whole((1, D_MODEL)), _whole((D_MODEL, D_IN)), _whole((D_MODEL, D_GATE)),
                  _whole((1, D_GATE))],
        out_specs=tuple(_rows(TM, s.shape[1]) for s in outs), out_shape=outs,
        compiler_params=_params(("parallel",)),
    )(x, g_pre, w_in, w_gate, b_gate)


def _ssm_scan_rows(n_rows, ar, ai, sign, in_re_ref, in_im_ref, out_re_ref, out_im_ref, carry_re_ref, carry_im_ref,
                   reverse):
    def group(gi, carry):
        hr, hi = carry
        g = (n_rows // 8 - 1 - gi) if reverse else gi
        r0 = pl.multiple_of(g * 8, 8)
        cr = in_re_ref[pl.ds(r0, 8), :]
        ci = in_im_ref[pl.ds(r0, 8), :]
        outs_r, outs_i = [None] * 8, [None] * 8
        for kk in range(8):
            k = 7 - kk if reverse else kk
            nr = ar * hr - sign * ai * hi + cr[k:k + 1, :]
            ni = ar * hi + sign * ai * hr + ci[k:k + 1, :]
            hr, hi = nr, ni
            outs_r[k], outs_i[k] = hr, hi
        out_re_ref[pl.ds(r0, 8), :] = jnp.concatenate(outs_r, axis=0)
        out_im_ref[pl.ds(r0, 8), :] = jnp.concatenate(outs_i, axis=0)
        return hr, hi

    hr, hi = lax.fori_loop(0, n_rows // 8, group, (carry_re_ref[...], carry_im_ref[...]))
    carry_re_ref[...] = hr
    carry_im_ref[...] = hi


def _ssm_fwd(u, bb_re, bb_im, ab_re, ab_im, c_re, c_im, d_skip):
    t = u.shape[0]

    def body(u_ref, bre_ref, bim_ref, ar_ref, ai_ref, cre_ref, cim_ref, d_ref, hre_ref, him_ref, y_ref,
             bur_ref, bui_ref, car_ref, cai_ref):
        @pl.when(pl.program_id(0) == 0)
        def _():
            car_ref[...] = jnp.zeros_like(car_ref)
            cai_ref[...] = jnp.zeros_like(cai_ref)

        uv = u_ref[...]
        ub = uv.astype(BF16)
        bur_ref[...] = _dot(ub, bre_ref[...])
        bui_ref[...] = _dot(ub, bim_ref[...])
        _ssm_scan_rows(TM, ar_ref[...], ai_ref[...], 1.0, bur_ref, bui_ref, hre_ref, him_ref, car_ref, cai_ref, False)
        y_ref[...] = (_dot(hre_ref[...].astype(BF16), cre_ref[...]) - _dot(him_ref[...].astype(BF16), cim_ref[...])
                      + d_ref[...] * uv)

    outs = (_sds((t, N_STATE), F32), _sds((t, N_STATE), F32), _sds((t, D_SSM), F32))
    return pl.pallas_call(
        body, name="ssm_fwd", grid=(t // TM,),
        in_specs=[_rows(TM, D_SSM), _whole((D_SSM, N_STATE)), _whole((D_SSM, N_STATE)), _whole((1, N_STATE)),
                  _whole((1, N_STATE)), _whole((N_STATE, D_SSM)), _whole((N_STATE, D_SSM)), _whole((1, D_SSM))],
        out_specs=(_rows(TM, N_STATE), _rows(TM, N_STATE), _rows(TM, D_SSM)), out_shape=outs,
        scratch_shapes=[pltpu.VMEM((TM, N_STATE), F32), pltpu.VMEM((TM, N_STATE), F32),
                        pltpu.VMEM((1, N_STATE), F32), pltpu.VMEM((1, N_STATE), F32)],
        compiler_params=_params(("arbitrary",)),
    )(u, bb_re, bb_im, ab_re, ab_im, c_re, c_im, d_skip)


def _neg_abs(x):
    return lax.bitcast_convert_type(lax.bitcast_convert_type(x, jnp.int32) | jnp.int32(-2 ** 31), F32)


def _sb_log1m(nz, mask):
    b = jnp.minimum(nz, 0.0) - jnp.log(1.0 + jnp.exp(_neg_abs(nz)))
    return b if mask is None else jnp.where(mask, b, 0.0)


def _sb_weights(nz, b, incl, r_run, mask):
    w = jnp.exp((r_run + _dot(b.astype(BF16), incl)) - nz)
    return w if mask is None else jnp.where(mask, w, 0.0)


def _rowsum(a):
    return jnp.sum(a, axis=1, keepdims=True)


def _attn_fwd(q_r, k_t, v_r):
    h, t, _ = q_r.shape
    nk = t // BQ
    assert nk <= R_LANES

    def body(q_ref, kt_ref, v_ref, o_ref, rtab_ref):
        i = pl.program_id(1)
        q = q_ref[0]
        rows = lax.broadcasted_iota(jnp.int32, (BQ, BQ), 0)
        cols = lax.broadcasted_iota(jnp.int32, (BQ, BQ), 1)
        tri = rows > cols
        incl = (rows >= cols).astype(BF16)
        lane = lax.broadcasted_iota(jnp.int32, (BQ, R_LANES), 1)

        def group(js, carry, mask=None):
            r_run, acc, rtab = carry
            nzs = [_dot(q, kt_ref[0, j]) for j in js]
            bs = [_sb_log1m(nz, mask) for nz in nzs]
            rs = [r_run]
            for b in bs:
                rs.append(rs[-1] + _rowsum(b))
            ws = [_sb_weights(nz, b, incl, r, mask) for nz, b, r in zip(nzs, bs, rs)]
            for j, w, r in zip(js, ws, rs):
                acc = acc + _dot(w.astype(BF16), v_ref[0, j])
                rtab = jnp.where(lane == j, r, rtab)
            return rs[-1], acc, rtab

        carry = (jnp.zeros((BQ, 1), F32), jnp.zeros((BQ, SB_HEAD_DIM), F32), jnp.zeros((BQ, R_LANES), F32))
        carry = group([i], carry, tri)
        rem = i % SB_UNROLL
        carry = lax.fori_loop(0, rem, lambda n, c: group([i - 1 - n], c), carry)
        carry = lax.fori_loop(0, i // SB_UNROLL,
                              lambda m, c: group([i - 1 - rem - SB_UNROLL * m - k for k in range(SB_UNROLL)], c), carry)
        o_ref[0] = carry[1]
        rtab_ref[0] = carry[2]

    return pl.pallas_call(
        body, name="attn_fwd", grid=(h, nk),
        in_specs=[pl.BlockSpec((1, BQ, SB_HEAD_DIM), lambda hh, i: (hh, i, 0)),
                  pl.BlockSpec((1, nk, SB_HEAD_DIM, BQ), lambda hh, i: (hh, 0, 0, 0)),
                  pl.BlockSpec((1, nk, BQ, SB_HEAD_DIM), lambda hh, i: (hh, 0, 0, 0))],
        out_specs=(pl.BlockSpec((1, BQ, SB_HEAD_DIM), lambda hh, i: (hh, i, 0)),
                   pl.BlockSpec((1, BQ, R_LANES), lambda hh, i: (hh, i, 0))),
        out_shape=(_sds((h, t, SB_HEAD_DIM), F32), _sds((h, t, R_LANES), F32)),
        compiler_params=_params(("parallel", "parallel")),
    )(q_r, k_t, v_r)


def _merge_fwd(y_pre, o, cb, cc, cx, conv_w, gate, x, w_val, w_glu, w_ao, w_co, w_mo, g_post):
    t = x.shape[0]

    def body(y_ref, o_ref, cb_ref, cc_ref, cx_ref, ccp_ref, cxp_ref, cw_ref, gate_ref, x_ref, wv_ref, wg_ref, wao_ref,
             wco_ref, wmo_ref, gp_ref, x1_ref, mg_ref, m2_ref):
        i = pl.program_id(0)
        ygb = _gelu(y_ref[...]).astype(BF16)
        y_a = _dot(ygb, wv_ref[...]) * _sigmoid(_dot(ygb, wg_ref[...]))
        y_b = _dot(o_ref[...], wao_ref[...])
        yconv = _conv_fwd(cc_ref[...] * cx_ref[...], jnp.where(i > 0, ccp_ref[...] * cxp_ref[...], 0.0), cw_ref[...])[0]
        y_c = _dot((cb_ref[...] * yconv).astype(BF16), wco_ref[...])
        gate_v = gate_ref[...]
        merged = (gate_v[:, :D_MODEL] * y_a + gate_v[:, D_MODEL:2 * D_MODEL] * y_b
                  + gate_v[:, 2 * D_MODEL:] * y_c).astype(BF16)
        mg_ref[...] = merged
        m2 = _dot(merged, wmo_ref[...])
        m2_ref[...] = m2
        x1_ref[...] = x_ref[...] + _rms_fwd(m2, gp_ref[...])

    outs = (_sds((t, D_MODEL), F32), _sds((t, D_MODEL), BF16), _sds((t, D_MODEL), F32))
    return pl.pallas_call(
        body, name="merge_fwd", grid=(t // TM,),
        in_specs=[_rows(TM, D_SSM), _rows(TM, D_SB), _rows(TM, D_CONV), _rows(TM, D_CONV), _rows(TM, D_CONV),
                  _halo_before(D_CONV), _halo_before(D_CONV), _whole((3, D_CONV)), _rows(TM, D_GATE),
                  _rows(TM, D_MODEL), _whole((D_SSM, D_MODEL)), _whole((D_SSM, D_MODEL)), _whole((D_SB, D_MODEL)),
                  _whole((D_CONV, D_MODEL)), _whole((D_MODEL, D_MODEL)), _whole((1, D_MODEL))],
        out_specs=tuple(_rows(TM, D_MODEL) for _ in outs), out_shape=outs,
        compiler_params=_params(("parallel",)),
    )(y_pre, o, cb, cc, cx, cc, cx, conv_w, gate, x, w_val, w_glu, w_ao, w_co, w_mo, g_post)


def _halo_before(n):
    return pl.BlockSpec((HALO, n), lambda i: (jnp.maximum(i * (TM // HALO) - 1, 0), 0))


def _halo_after(n, n_tiles):
    return pl.BlockSpec((HALO, n), lambda i: (jnp.minimum((i + 1) * (TM // HALO), n_tiles * (TM // HALO) - 1), 0))


def _conv_fwd(z, z_before, cw):
    row = lax.broadcasted_iota(jnp.int32, z.shape, 0)
    z1 = jnp.where(row == 0, z_before[HALO - 1:HALO, :], pltpu.roll(z, 1, axis=0))
    z2 = jnp.where(row == 0, z_before[HALO - 2:HALO - 1, :],
                   jnp.where(row == 1, z_before[HALO - 1:HALO, :], pltpu.roll(z, 2, axis=0)))
    return cw[0:1, :] * z2 + cw[1:2, :] * z1 + cw[2:3, :] * z, z1, z2


def _ffn_fwd(x1, g1, g2, w_up, w_down):
    t = x1.shape[0]

    def body(x_ref, g1_ref, g2_ref, wu_ref, wd_ref, x2_ref, h2_ref, f_ref):
        xv = x_ref[...]
        h2 = _rms_fwd(xv, g1_ref[...]).astype(BF16)
        h2_ref[...] = h2
        ra = jnp.maximum(_dot(h2, wu_ref[...]), 0.0)
        f = _dot((ra * ra).astype(BF16), wd_ref[...])
        f_ref[...] = f
        x2_ref[...] = xv + _rms_fwd(f, g2_ref[...])

    outs = (_sds((t, D_MODEL), F32), _sds((t, D_MODEL), BF16), _sds((t, D_MODEL), F32))
    return pl.pallas_call(
        body, name="ffn_fwd", grid=(t // TM,),
        in_specs=[_rows(TM, D_MODEL), _whole((1, D_MODEL)), _whole((1, D_MODEL)), _whole((D_MODEL, D_FF)),
                  _whole((D_FF, D_MODEL))],
        out_specs=tuple(_rows(TM, D_MODEL) for _ in outs), out_shape=outs,
        compiler_params=_params(("parallel",)),
    )(x1, g1, g2, w_up, w_down)


def _loss_grad(y, target):
    t = y.shape[0]

    def body(y_ref, t_ref, dy_ref, loss_ref):
        @pl.when(pl.program_id(0) == 0)
        def _():
            loss_ref[...] = jnp.zeros_like(loss_ref)

        err = y_ref[...] - t_ref[...]
        dy_ref[...] = err * (1.0 / D_MODEL)
        loss_ref[...] += 0.5 * jnp.sum(jnp.mean(err * err, axis=-1, keepdims=True), axis=0, keepdims=True)

    return pl.pallas_call(
        body, name="loss_grad", grid=(t // TM,),
        in_specs=[_rows(TM, D_MODEL), _rows(TM, D_MODEL)],
        out_specs=(_rows(TM, D_MODEL), _whole((8, 128))),
        out_shape=(_sds((t, D_MODEL), F32), _sds((8, 128), F32)),
        compiler_params=_params(("arbitrary",)),
    )(y, target)


def _ffn_bwd(dx2, f, h2, x1, g1, g2, w_up, w_up_t, w_down_t):
    t = x1.shape[0]

    def body(dx2_ref, f_ref, h2_ref, x1_ref, g1_ref, g2_ref, wu_ref, wut_ref, wdt_ref, dx1_ref, da_ref, r_ref, df_ref,
             dg1_ref, dg2_ref):
        @pl.when(pl.program_id(0) == 0)
        def _():
            dg1_ref[...] = jnp.zeros_like(dg1_ref)
            dg2_ref[...] = jnp.zeros_like(dg2_ref)

        dx2 = dx2_ref[...]
        df, dg2 = _rms_bwd(f_ref[...], g2_ref[...], dx2)
        dg2_ref[...] += _colsum(dg2)
        dfb = df.astype(BF16)
        df_ref[...] = dfb
        ra = jnp.maximum(_dot(h2_ref[...], wu_ref[...]), 0.0)
        r_ref[...] = (ra * ra).astype(BF16)
        da = (_dot(dfb, wdt_ref[...]) * (2.0 * ra)).astype(BF16)
        da_ref[...] = da
        dxn, dg1 = _rms_bwd(x1_ref[...], g1_ref[...], _dot(da, wut_ref[...]))
        dg1_ref[...] += _colsum(dg1)
        dx1_ref[...] = dx2 + dxn

    outs = (_sds((t, D_MODEL), F32), _sds((t, D_FF), BF16), _sds((t, D_FF), BF16), _sds((t, D_MODEL), BF16),
            _sds((1, D_MODEL), F32), _sds((1, D_MODEL), F32))
    return pl.pallas_call(
        body, name="ffn_bwd", grid=(t // TM,),
        in_specs=[_rows(TM, D_MODEL), _rows(TM, D_MODEL), _rows(TM, D_MODEL), _rows(TM, D_MODEL), _whole((1, D_MODEL)),
                  _whole((1, D_MODEL)), _whole((D_MODEL, D_FF)), _whole((D_FF, D_MODEL)), _whole((D_MODEL, D_FF))],
        out_specs=(_rows(TM, D_MODEL), _rows(TM, D_FF), _rows(TM, D_FF), _rows(TM, D_MODEL), _whole((1, D_MODEL)),
                   _whole((1, D_MODEL))),
        out_shape=outs, compiler_params=_params(("arbitrary",), 60),
    )(dx2, f, h2, x1, g1, g2, w_up, w_up_t, w_down_t)


def _mm_tn(a, b):
    t, k = a.shape
    n = b.shape[1]
    tk, tt = min(k, 512), min(t, 512)
    tn = next(c for c in (1024, 512, 256) if n % c == 0)

    def body(a_ref, b_ref, o_ref):
        @pl.when(pl.program_id(2) == 0)
        def _():
            o_ref[...] = jnp.zeros_like(o_ref)

        o_ref[...] += _dot_tn(a_ref[...], b_ref[...])

    return pl.pallas_call(
        body, name="mm_tn", grid=(k // tk, n // tn, t // tt),
        in_specs=[pl.BlockSpec((tt, tk), lambda i, j, s: (s, i)), pl.BlockSpec((tt, tn), lambda i, j, s: (s, j))],
        out_specs=pl.BlockSpec((tk, tn), lambda i, j, s: (i, j)), out_shape=_sds((k, n), F32),
        compiler_params=_params(("parallel", "parallel", "arbitrary")),
    )(a, b)


def _merge_bwd(dx1, m2, g_post, w_mo_t, gate, y_pre, o, cb, cc, cx, conv_w, w_val, w_glu, w_ao, w_co, w_val_t, w_glu_t,
               w_ao_t, w_co_t):
    t = dx1.shape[0]

    def body(dx1_ref, m2_ref, gp_ref, wmot_ref, gate_ref, y_ref, o_ref, cb_ref, cc_ref, cx_ref, ccp_ref, cxp_ref, cw_ref,
             wv_ref, wg_ref, wao_ref, wco_ref, wvt_ref, wgt_ref, waot_ref, wcot_ref,
             dgp_ref, dyp_ref, do_ref, dcb_ref, dyc_ref, dm2_ref, da_ref, dbg_ref, yg_ref, dyb_ref, dycc_ref, cy_ref,
             dgpost_ref, dbgate_ref):
        i = pl.program_id(0)

        @pl.when(i == 0)
        def _():
            dgpost_ref[...] = jnp.zeros_like(dgpost_ref)
            dbgate_ref[...] = jnp.zeros_like(dbgate_ref)

        dm2, dgpost = _rms_bwd(m2_ref[...], gp_ref[...], dx1_ref[...])
        dgpost_ref[...] += _colsum(dgpost)
        dm2b = dm2.astype(BF16)
        dm2_ref[...] = dm2b
        dmerged = _dot(dm2b, wmot_ref[...])

        yv = y_ref[...]
        ygb = _gelu(yv).astype(BF16)
        yg_ref[...] = ygb
        a_val = _dot(ygb, wv_ref[...])
        s_glu = _sigmoid(_dot(ygb, wg_ref[...]))
        y_a = a_val * s_glu
        y_b = _dot(o_ref[...], wao_ref[...])
        cbv = cb_ref[...]
        yconv = _conv_fwd(cc_ref[...] * cx_ref[...], jnp.where(i > 0, ccp_ref[...] * cxp_ref[...], 0.0), cw_ref[...])[0]
        cyb = (cbv * yconv).astype(BF16)
        cy_ref[...] = cyb
        y_c = _dot(cyb, wco_ref[...])

        gate_v = gate_ref[...]
        g_a, g_b, g_c = gate_v[:, :D_MODEL], gate_v[:, D_MODEL:2 * D_MODEL], gate_v[:, 2 * D_MODEL:]
        dgp = jnp.concatenate([dmerged * y_a * g_a * (1.0 - g_a), dmerged * y_b * g_b * (1.0 - g_b),
                               dmerged * y_c * g_c * (1.0 - g_c)], axis=1)
        dbgate_ref[...] += _colsum(dgp)
        dgp_ref[...] = dgp.astype(BF16)

        dy_a = dmerged * g_a
        d_val = (dy_a * s_glu).astype(BF16)
        d_glu = (dy_a * a_val * s_glu * (1.0 - s_glu)).astype(BF16)
        da_ref[...] = d_val
        dbg_ref[...] = d_glu
        dyp_ref[...] = (_dot(d_val, wvt_ref[...]) + _dot(d_glu, wgt_ref[...])) * _gelu_grad(yv)

        dy_b = (dmerged * g_b).astype(BF16)
        dyb_ref[...] = dy_b
        do_ref[...] = _dot(dy_b, waot_ref[...]).astype(BF16)

        dy_c = (dmerged * g_c).astype(BF16)
        dycc_ref[...] = dy_c
        dcy = _dot(dy_c, wcot_ref[...])
        dcb_ref[...] = dcy * yconv
        dyc_ref[...] = dcy * cbv

    outs = (_sds((t, D_GATE), BF16), _sds((t, D_SSM), F32), _sds((t, D_SB), BF16), _sds((t, D_CONV), F32),
            _sds((t, D_CONV), F32), _sds((t, D_MODEL), BF16), _sds((t, D_MODEL), BF16), _sds((t, D_MODEL), BF16),
            _sds((t, D_SSM), BF16), _sds((t, D_MODEL), BF16), _sds((t, D_MODEL), BF16), _sds((t, D_CONV), BF16),
            _sds((1, D_MODEL), F32), _sds((1, D_GATE), F32))
    out_specs = tuple(_rows(TM, s.shape[1]) for s in outs[:-2]) + (_whole((1, D_MODEL)), _whole((1, D_GATE)))
    return pl.pallas_call(
        body, name="merge_bwd", grid=(t // TM,),
        in_specs=[_rows(TM, D_MODEL), _rows(TM, D_MODEL), _whole((1, D_MODEL)), _whole((D_MODEL, D_MODEL)),
                  _rows(TM, D_GATE), _rows(TM, D_SSM), _rows(TM, D_SB), _rows(TM, D_CONV), _rows(TM, D_CONV),
                  _rows(TM, D_CONV), _halo_before(D_CONV), _halo_before(D_CONV), _whole((3, D_CONV)),
                  _whole((D_SSM, D_MODEL)), _whole((D_SSM, D_MODEL)), _whole((D_SB, D_MODEL)),
                  _whole((D_CONV, D_MODEL)), _whole((D_MODEL, D_SSM)), _whole((D_MODEL, D_SSM)),
                  _whole((D_MODEL, D_SB)), _whole((D_MODEL, D_CONV))],
        out_specs=out_specs, out_shape=outs, compiler_params=_params(("arbitrary",)),
    )(dx1, m2, g_post, w_mo_t, gate, y_pre, o, cb, cc, cx, cc, cx, conv_w, w_val, w_glu, w_ao, w_co, w_val_t, w_glu_t,
      w_ao_t, w_co_t)


def _conv_bwd(dyconv, cc, cx, conv_w):
    t = dyconv.shape[0]
    nt = t // TM

    def body(dy_ref, dya_ref, cc_ref, cx_ref, ccp_ref, cxp_ref, cw_ref, dcc_ref, dcx_ref, dcw_ref):
        i = pl.program_id(0)

        @pl.when(i == 0)
        def _():
            dcw_ref[...] = jnp.zeros_like(dcw_ref)

        dy = dy_ref[...]
        dy_after = jnp.where(i < nt - 1, dya_ref[...], 0.0)
        row = lax.broadcasted_iota(jnp.int32, dy.shape, 0)
        dy1 = jnp.where(row == TM - 1, dy_after[0:1, :], pltpu.roll(dy, TM - 1, axis=0))
        dy2 = jnp.where(row == TM - 1, dy_after[1:2, :],
                        jnp.where(row == TM - 2, dy_after[0:1, :], pltpu.roll(dy, TM - 2, axis=0)))
        cw = cw_ref[...]
        dz = cw[2:3, :] * dy + cw[1:2, :] * dy1 + cw[0:1, :] * dy2
        ccv, cxv = cc_ref[...], cx_ref[...]
        dcc_ref[...] = dz * cxv
        dcx_ref[...] = dz * ccv
        z = ccv * cxv
        _, z1, z2 = _conv_fwd(z, jnp.where(i > 0, ccp_ref[...] * cxp_ref[...], 0.0), cw)
        dcw_ref[0:1, :] += _colsum(dy * z2)
        dcw_ref[1:2, :] += _colsum(dy * z1)
        dcw_ref[2:3, :] += _colsum(dy * z)

    return pl.pallas_call(
        body, name="conv_bwd", grid=(nt,),
        in_specs=[_rows(TM, D_CONV), _halo_after(D_CONV, nt), _rows(TM, D_CONV), _rows(TM, D_CONV),
                  _halo_before(D_CONV), _halo_before(D_CONV), _whole((3, D_CONV))],
        out_specs=(_rows(TM, D_CONV), _rows(TM, D_CONV), _whole((8, D_CONV))),
        out_shape=(_sds((t, D_CONV), F32), _sds((t, D_CONV), F32), _sds((8, D_CONV), F32)),
        compiler_params=_params(("arbitrary",)),
    )(dyconv, dyconv, cc, cx, cc, cx, conv_w)


def _attn_bwd(q_r, q_t, do_r, do_t, k_t, k_r, v_t, r_tab):
    h, t, _ = q_r.shape
    nk = t // BQ

    def body(q_ref, qt_ref, do_ref, dot_ref, kt_ref, kr_ref, vt_ref, rtab_ref, dq_ref, dkt_ref, dvt_ref):
        i = pl.program_id(1)

        @pl.when(i == 0)
        def _():
            dkt_ref[...] = jnp.zeros_like(dkt_ref)
            dvt_ref[...] = jnp.zeros_like(dvt_ref)

        q, q_tr, do, do_tr, rtab = q_ref[0], qt_ref[0, 0], do_ref[0], dot_ref[0, 0], rtab_ref[0]
        rows = lax.broadcasted_iota(jnp.int32, (BQ, BQ), 0)
        cols = lax.broadcasted_iota(jnp.int32, (BQ, BQ), 1)
        tri = rows > cols
        incl = (rows >= cols).astype(BF16)
        lower = (rows < cols).astype(BF16)
        lane = lax.broadcasted_iota(jnp.int32, (BQ, R_LANES), 1)

        def scores(j, mask):
            nz = _dot(q, kt_ref[0, j])
            b = _sb_log1m(nz, mask)
            w = _sb_weights(nz, b, incl, _rowsum(jnp.where(lane == j, rtab, 0.0)), mask)
            return b, w, w * _dot(do, vt_ref[0, j])

        def finish(j, b, w, e, p_run, dq, mask):
            p = p_run + _dot(e.astype(BF16), lower)
            dz = (e + p) * jnp.exp(b) - p
            if mask is not None:
                dz = jnp.where(mask, dz, 0.0)
            dzb = dz.astype(BF16)
            dvt_ref[0, j] += _dot(do_tr, w.astype(BF16))
            dkt_ref[0, j] -= _dot(q_tr, dzb)
            return dq + _dot(dzb, kr_ref[0, j])

        def group(js, carry, mask=None):
            p_run, dq = carry
            parts = [scores(j, mask) for j in js]
            ps = [p_run]
            for _, _, e in parts:
                ps.append(ps[-1] + _rowsum(e))
            for j, (b, w, e), p in zip(js, parts, ps):
                dq = finish(j, b, w, e, p, dq, mask)
            return ps[-1], dq

        carry = (jnp.zeros((BQ, 1), F32), jnp.zeros((BQ, SB_HEAD_DIM), F32))
        full = i // SB_UNROLL
        carry = lax.fori_loop(0, full, lambda m, c: group([SB_UNROLL * m + k for k in range(SB_UNROLL)], c), carry)
        carry = lax.fori_loop(0, i % SB_UNROLL, lambda n, c: group([SB_UNROLL * full + n], c), carry)
        dq_ref[0] = group([i], carry, tri)[1] * SB_SCALE

    row_blk = pl.BlockSpec((1, BQ, SB_HEAD_DIM), lambda hh, i: (hh, i, 0))
    col_blk = pl.BlockSpec((1, 1, SB_HEAD_DIM, BQ), lambda hh, i: (hh, i, 0, 0))
    head_t = pl.BlockSpec((1, nk, SB_HEAD_DIM, BQ), lambda hh, i: (hh, 0, 0, 0))
    head_r = pl.BlockSpec((1, nk, BQ, SB_HEAD_DIM), lambda hh, i: (hh, 0, 0, 0))
    return pl.pallas_call(
        body, name="attn_bwd", grid=(h, nk),
        in_specs=[row_blk, col_blk, row_blk, col_blk, head_t, head_r, head_t,
                  pl.BlockSpec((1, BQ, R_LANES), lambda hh, i: (hh, i, 0))],
        out_specs=(row_blk, head_t, head_t),
        out_shape=(_sds((h, t, SB_HEAD_DIM), F32), _sds((h, nk, SB_HEAD_DIM, BQ), F32),
                   _sds((h, nk, SB_HEAD_DIM, BQ), F32)),
        compiler_params=_params(("parallel", "arbitrary")),
    )(q_r, q_t, do_r, do_t, k_t, k_r, v_t, r_tab)


def _ssm_bwd(dy, u, h_re, h_im, ct_re, ct_im, bt_re, bt_im, ab_re, ab_im, d_skip):
    t = u.shape[0]
    nt = t // TM

    def rev(n):
        return pl.BlockSpec((TM, n), lambda i: (nt - 1 - i, 0))

    def before(n):
        return pl.BlockSpec((HALO, n), lambda i: (jnp.maximum((nt - 1 - i) * (TM // HALO) - 1, 0), 0))

    def body(dy_ref, u_ref, hre_ref, him_ref, hrp_ref, hip_ref, ctre_ref, ctim_ref, btre_ref, btim_ref, ar_ref, ai_ref,
             d_ref, du_ref, dbre_ref, dbim_ref, dcre_ref, dcim_ref, dar_ref, dai_ref, dd_ref,
             ghr_ref, ghi_ref, lamr_ref, lami_ref, car_ref, cai_ref):
        i = pl.program_id(0)

        @pl.when(i == 0)
        def _():
            for ref in (car_ref, cai_ref, dbre_ref, dbim_ref, dcre_ref, dcim_ref, dar_ref, dai_ref, dd_ref):
                ref[...] = jnp.zeros_like(ref)

        dyv = dy_ref[...]
        dyb = dyv.astype(BF16)
        ghr_ref[...] = _dot(dyb, ctre_ref[...])
        ghi_ref[...] = -_dot(dyb, ctim_ref[...])
        _ssm_scan_rows(TM, ar_ref[...], ai_ref[...], -1.0, ghr_ref, ghi_ref, lamr_ref, lami_ref, car_ref, cai_ref, True)
        lam_r, lam_i = lamr_ref[...], lami_ref[...]
        lam_rb, lam_ib = lam_r.astype(BF16), lam_i.astype(BF16)
        uv = u_ref[...]
        ub = uv.astype(BF16)
        du_ref[...] = _dot(lam_rb, btre_ref[...]) + _dot(lam_ib, btim_ref[...]) + d_ref[...] * dyv
        dbre_ref[...] += _dot_tn(ub, lam_rb)
        dbim_ref[...] += _dot_tn(ub, lam_ib)
        h_r, h_i = hre_ref[...], him_ref[...]
        dcre_ref[...] += _dot_tn(dyb, h_r.astype(BF16))
        dcim_ref[...] -= _dot_tn(dyb, h_i.astype(BF16))
        first = i == nt - 1
        row = lax.broadcasted_iota(jnp.int32, h_r.shape, 0)
        h_r1 = jnp.where(row == 0, jnp.where(first, 0.0, hrp_ref[HALO - 1:HALO, :]), pltpu.roll(h_r, 1, axis=0))
        h_i1 = jnp.where(row == 0, jnp.where(first, 0.0, hip_ref[HALO - 1:HALO, :]), pltpu.roll(h_i, 1, axis=0))
        dar_ref[...] += _colsum(lam_r * h_r1 + lam_i * h_i1)
        dai_ref[...] += _colsum(lam_i * h_r1 - lam_r * h_i1)
        dd_ref[...] += _colsum(dyv * uv)

    outs = (_sds((t, D_SSM), F32), _sds((D_SSM, N_STATE), F32), _sds((D_SSM, N_STATE), F32),
            _sds((D_SSM, N_STATE), F32), _sds((D_SSM, N_STATE), F32), _sds((1, N_STATE), F32),
            _sds((1, N_STATE), F32), _sds((1, D_SSM), F32))
    return pl.pallas_call(
        body, name="ssm_bwd", grid=(nt,),
        in_specs=[rev(D_SSM), rev(D_SSM), rev(N_STATE), rev(N_STATE), before(N_STATE), before(N_STATE),
                  _whole((D_SSM, N_STATE)), _whole((D_SSM, N_STATE)), _whole((N_STATE, D_SSM)),
                  _whole((N_STATE, D_SSM)), _whole((1, N_STATE)), _whole((1, N_STATE)), _whole((1, D_SSM))],
        out_specs=(rev(D_SSM),) + tuple(_whole(s.shape) for s in outs[1:]), out_shape=outs,
        scratch_shapes=[pltpu.VMEM((TM, N_STATE), F32) for _ in range(4)]
        + [pltpu.VMEM((1, N_STATE), F32), pltpu.VMEM((1, N_STATE), F32)],
        compiler_params=_params(("arbitrary",)),
    )(dy, u, h_re, h_im, h_re, h_im, ct_re, ct_im, bt_re, bt_im, ab_re, ab_im, d_skip)


def _in_bwd(dp, dgp, w_in_t, w_gate_t, x, g_pre, dx_res):
    t = x.shape[0]

    def body(dp_ref, dgp_ref, wit_ref, wgt_ref, x_ref, g_ref, dxr_ref, dx_ref, dg_ref):
        @pl.when(pl.program_id(0) == 0)
        def _():
            dg_ref[...] = jnp.zeros_like(dg_ref)

        dh = _dot(dp_ref[...], wit_ref[...]) + _dot(dgp_ref[...], wgt_ref[...])
        dxn, dg = _rms_bwd(x_ref[...], g_ref[...], dh)
        dg_ref[...] += _colsum(dg)
        dx_ref[...] = dxr_ref[...] + dxn

    return pl.pallas_call(
        body, name="in_bwd", grid=(t // TM,),
        in_specs=[_rows(TM, D_IN), _rows(TM, D_GATE), _whole((D_IN, D_MODEL)), _whole((D_GATE, D_MODEL)),
                  _rows(TM, D_MODEL), _whole((1, D_MODEL)), _rows(TM, D_MODEL)],
        out_specs=(_rows(TM, D_MODEL), _whole((1, D_MODEL))),
        out_shape=(_sds((t, D_MODEL), F32), _sds((1, D_MODEL), F32)),
        compiler_params=_params(("arbitrary",)),
    )(dp, dgp, w_in_t, w_gate_t, x, g_pre, dx_res)


def _ssm_disc(lr, li, ldt):
    dt = jnp.exp(ldt)
    mag = jnp.exp(lr * dt)
    th = li * dt
    cs, sn = jnp.cos(th), jnp.sin(th)
    ab_re, ab_im = mag * cs, mag * sn
    den = lr * lr + li * li
    xr = ab_re - 1.0
    co_re = (xr * lr + ab_im * li) / den
    co_im = (ab_im * lr - xr * li) / den
    return dt, mag, cs, sn, ab_re, ab_im, den, xr, co_re, co_im


def _ssm_param_fwd(lr, li, ldt, b_re_t, b_im_t):
    def body(lr_ref, li_ref, ldt_ref, br_ref, bi_ref, are_ref, aim_ref, bbr_ref, bbi_ref):
        _, _, _, _, ab_re, ab_im, _, _, co_re, co_im = _ssm_disc(lr_ref[...], li_ref[...], ldt_ref[...])
        are_ref[...] = ab_re
        aim_ref[...] = ab_im
        br, bi = br_ref[...], bi_ref[...]
        bbr_ref[...] = co_re * br - co_im * bi
        bbi_ref[...] = co_re * bi + co_im * br

    vec, mat = _sds((1, N_STATE), F32), _sds((SSM_GROUP, N_STATE), F32)
    return pl.pallas_call(body, name="ssm_param_fwd", out_shape=(vec, vec, mat, mat))(lr, li, ldt, b_re_t, b_im_t)


def _ssm_param_bwd(lr, li, ldt, b_re_t, b_im_t, dab_re, dab_im, dbb_re_t, dbb_im_t):
    def body(lr_ref, li_ref, ldt_ref, br_ref, bi_ref, dar_ref, dai_ref, dbbr_ref, dbbi_ref,
             dlr_ref, dli_ref, dldt_ref, dbr_ref, dbi_ref):
        lr, li = lr_ref[...], li_ref[...]
        dt, mag, cs, sn, ab_re, ab_im, den, xr, co_re, co_im = _ssm_disc(lr, li, ldt_ref[...])
        br, bi, dbbr, dbbi = br_ref[...], bi_ref[...], dbbr_ref[...], dbbi_ref[...]
        dbr_ref[...] = co_re * dbbr + co_im * dbbi
        dbi_ref[...] = co_re * dbbi - co_im * dbbr
        dco_re = _colsum(br * dbbr + bi * dbbi)
        dco_im = _colsum(br * dbbi - bi * dbbr)
        dxr = (dco_re * lr - dco_im * li) / den
        dab_i = dai_ref[...] + (dco_re * li + dco_im * lr) / den
        dab_r = dar_ref[...] + dxr
        dden = -(co_re * dco_re + co_im * dco_im) / den
        dlr = (dco_re * xr + dco_im * ab_im) / den + dden * 2.0 * lr
        dli = (dco_re * ab_im - dco_im * xr) / den + dden * 2.0 * li
        dmag = dab_r * cs + dab_i * sn
        dth = mag * (dab_i * cs - dab_r * sn)
        dlr_ref[...] = dlr + dmag * mag * dt
        dli_ref[...] = dli + dth * dt
        dldt = jnp.broadcast_to((dmag * mag * lr + dth * li) * dt, (8, N_STATE))
        group = (lax.broadcasted_iota(jnp.int32, (N_STATE, 128), 0) // SSM_STATE
                 == lax.broadcasted_iota(jnp.int32, (N_STATE, 128), 1)).astype(F32)
        dldt_ref[...] = jnp.dot(dldt, group, precision=lax.Precision.HIGHEST, preferred_element_type=F32)

    vec, mat = _sds((1, N_STATE), F32), _sds((SSM_GROUP, N_STATE), F32)
    return pl.pallas_call(body, name="ssm_param_bwd", out_shape=(vec, vec, _sds((8, 128), F32), mat, mat))(
        lr, li, ldt, b_re_t, b_im_t, dab_re, dab_im, dbb_re_t, dbb_im_t)


ANY = pl.BlockSpec(memory_space=pl.ANY)


def _chip_peers(x, y):
    return [(1 - x, y), (x, 1 - y), (1 - x, 1 - y)]


def _gather_chips(buf):
    def body(src_ref, out_ref, send_sems, recv_sems, local_sem):
        x, y, c = lax.axis_index("x"), lax.axis_index("y"), lax.axis_index("c")
        local = pltpu.make_async_copy(src_ref, out_ref.at[2 * x + y], local_sem)
        local.start()
        sends = []
        for k, (px, py) in enumerate(_chip_peers(x, y)):
            sends.append(pltpu.make_async_remote_copy(src_ref, out_ref.at[2 * x + y], send_sems.at[k], recv_sems.at[k],
                                                      device_id=(px, py, c), device_id_type=MESH))
            sends[-1].start()
        for k, (px, py) in enumerate(_chip_peers(x, y)):
            pltpu.make_async_remote_copy(src_ref, out_ref.at[2 * px + py], send_sems.at[k], recv_sems.at[k],
                                         device_id=(px, py, c), device_id_type=MESH).wait_recv()
        for cp in sends:
            cp.wait_send()
        local.wait()

    return pl.pallas_call(
        body, name="gather_chips", in_specs=[ANY], out_specs=ANY, out_shape=_sds((N_CHIPS,) + buf.shape, buf.dtype),
        scratch_shapes=[pltpu.SemaphoreType.DMA((3,)), pltpu.SemaphoreType.DMA((3,)), pltpu.SemaphoreType.DMA],
    )(buf)


def _scatter_chips(buf):
    def body(src_ref, out_ref, send_sems, recv_sems, local_sem):
        x, y, c = lax.axis_index("x"), lax.axis_index("y"), lax.axis_index("c")
        me = 2 * x + y
        local = pltpu.make_async_copy(src_ref.at[me], out_ref.at[me], local_sem)
        local.start()
        sends = []
        for k, (px, py) in enumerate(_chip_peers(x, y)):
            sends.append(pltpu.make_async_remote_copy(src_ref.at[2 * px + py], out_ref.at[me], send_sems.at[k],
                                                      recv_sems.at[k], device_id=(px, py, c), device_id_type=MESH))
            sends[-1].start()
        for k, (px, py) in enumerate(_chip_peers(x, y)):
            pltpu.make_async_remote_copy(src_ref.at[me], out_ref.at[2 * px + py], send_sems.at[k], recv_sems.at[k],
                                         device_id=(px, py, c), device_id_type=MESH).wait_recv()
        for cp in sends:
            cp.wait_send()
        local.wait()

    return pl.pallas_call(
        body, name="scatter_chips", in_specs=[ANY], out_specs=ANY, out_shape=_sds(buf.shape, buf.dtype),
        scratch_shapes=[pltpu.SemaphoreType.DMA((3,)), pltpu.SemaphoreType.DMA((3,)), pltpu.SemaphoreType.DMA],
    )(buf)


SWAP_CHUNKS = 8


def _swap_cores(buf):
    rows = buf.shape[0] // SWAP_CHUNKS

    def body(src_ref, out_ref, send_sems, recv_sems):
        x, y, c = lax.axis_index("x"), lax.axis_index("y"), lax.axis_index("c")
        copies = []
        for k in range(SWAP_CHUNKS):
            chunk = pl.ds(k * rows, rows)
            copies.append(pltpu.make_async_remote_copy(src_ref.at[chunk], out_ref.at[chunk], send_sems.at[k],
                                                       recv_sems.at[k], device_id=(x, y, 1 - c), device_id_type=MESH))
            copies[-1].start()
        for cp in copies:
            cp.wait_recv()
        for cp in copies:
            cp.wait_send()

    return pl.pallas_call(
        body, name="swap_cores", in_specs=[ANY], out_specs=ANY, out_shape=_sds(buf.shape, buf.dtype),
        scratch_shapes=[pltpu.SemaphoreType.DMA((SWAP_CHUNKS,)), pltpu.SemaphoreType.DMA((SWAP_CHUNKS,))],
    )(buf)


def _gather_all(buf):
    flips = [(dx, dy, dc) for dx in (0, 1) for dy in (0, 1) for dc in (0, 1)][1:]

    def body(src_ref, out_ref, send_sems, recv_sems, local_sem):
        x, y, c = lax.axis_index("x"), lax.axis_index("y"), lax.axis_index("c")
        me = 4 * x + 2 * y + c
        peers = [(1 - x if dx else x, 1 - y if dy else y, 1 - c if dc else c) for dx, dy, dc in flips]
        local = pltpu.make_async_copy(src_ref, out_ref.at[me], local_sem)
        local.start()
        sends = []
        for k, peer in enumerate(peers):
            sends.append(pltpu.make_async_remote_copy(src_ref, out_ref.at[me], send_sems.at[k], recv_sems.at[k],
                                                      device_id=peer, device_id_type=MESH))
            sends[-1].start()
        for k, (px, py, pc) in enumerate(peers):
            pltpu.make_async_remote_copy(src_ref, out_ref.at[4 * px + 2 * py + pc], send_sems.at[k], recv_sems.at[k],
                                         device_id=(px, py, pc), device_id_type=MESH).wait_recv()
        for cp in sends:
            cp.wait_send()
        local.wait()

    return pl.pallas_call(
        body, name="gather_all", in_specs=[ANY], out_specs=ANY, out_shape=_sds((N_DEV,) + buf.shape, buf.dtype),
        scratch_shapes=[pltpu.SemaphoreType.DMA((7,)), pltpu.SemaphoreType.DMA((7,)), pltpu.SemaphoreType.DMA],
    )(buf)


PACK_W = 1024
PACK_ROWS = 256


def _sum_parts(parts):
    n, r, w = parts.shape

    def body(p_ref, o_ref):
        acc = p_ref[0].astype(F32)
        for k in range(1, n):
            acc = acc + p_ref[k].astype(F32)
        o_ref[...] = acc

    return pl.pallas_call(
        body, name="sum_parts", grid=(r // PACK_ROWS,),
        in_specs=[pl.BlockSpec((n, PACK_ROWS, w), lambda i: (0, i, 0))], out_specs=_rows(PACK_ROWS, w),
        out_shape=_sds((r, w), F32), compiler_params=_params(("parallel",)),
    )(parts)


def _adamw(parts, w, m, v):
    n = len(parts)
    r, wd = w.shape

    def body(*refs):
        w_ref, m_ref, v_ref, g_ref, dw_ref, nm_ref, nv_ref = refs[n:]
        g = refs[0][...]
        for k in range(1, n):
            g = g + refs[k][...]
        g_ref[...] = g
        m_new = ADAM_B1 * m_ref[...] + (1.0 - ADAM_B1) * g
        v_new = ADAM_B2 * v_ref[...] + (1.0 - ADAM_B2) * (g * g)
        nm_ref[...] = m_new
        nv_ref[...] = v_new
        m_hat = m_new / (1.0 - ADAM_B1 ** ADAM_STEP)
        v_hat = v_new / (1.0 - ADAM_B2 ** ADAM_STEP)
        dw_ref[...] = -ADAM_LR * (m_hat / (jnp.sqrt(v_hat) + ADAM_EPS) + ADAM_WD * w_ref[...])

    blk = _rows(PACK_ROWS, wd)
    return pl.pallas_call(
        body, name="adamw", grid=(r // PACK_ROWS,),
        in_specs=[blk] * (n + 3), out_specs=(blk,) * 4,
        out_shape=(_sds((r, wd), F32),) * 4, compiler_params=_params(("parallel",)),
    )(*parts, w, m, v)


PACK_UNIT = PACK_W * 16


def _pack(arrs, dtype):
    unit = PACK_UNIT
    flat = []
    for a in arrs:
        a = a.reshape(-1).astype(dtype)
        flat.append(jnp.pad(a, (0, -a.size % unit)))
    total = sum(f.size for f in flat)
    flat.append(jnp.zeros((-total % (PACK_W * PACK_ROWS),), dtype))
    return jnp.concatenate(flat).reshape(-1, PACK_W)


def _unpack(buf, shapes):
    unit = PACK_UNIT
    flat = buf.reshape(-1)
    out, off = [], 0
    for shp in shapes:
        size = 1
        for s in shp:
            size *= s
        out.append(flat[off:off + size].reshape(shp))
        off += size + (-size % unit)
    return out


def _expand_bd(m_t):
    g_row = jnp.arange(D_SSM)[:, None] // SSM_GROUP
    g_col = jnp.arange(N_STATE)[None, :] // SSM_STATE
    return jnp.where(g_row == g_col, jnp.tile(m_t, (SSM_GROUPS, 1)), 0.0)


def _extract_bd(full):
    g_row = jnp.arange(D_SSM)[:, None] // SSM_GROUP
    g_col = jnp.arange(N_STATE)[None, :] // SSM_STATE
    return jnp.where(g_row == g_col, full, 0.0).reshape(SSM_GROUPS, SSM_GROUP, N_STATE).sum(0)


def _heads_rows(a):
    return a.reshape(a.shape[0], SB_HEADS, SB_HEAD_DIM).transpose(1, 0, 2)


def _heads_blocks_t(a):
    return a.reshape(a.shape[0] // BQ, BQ, SB_HEADS, SB_HEAD_DIM).transpose(2, 0, 3, 1)


def _heads_blocks_r(a):
    return a.reshape(a.shape[0] // BQ, BQ, SB_HEADS, SB_HEAD_DIM).transpose(2, 0, 1, 3)


def _prep_layer(w, l):
    p = {}
    for name in ('norm_mix_pre', 'norm_mix_post', 'b_gate', 'norm_ffn_pre', 'norm_ffn_post', 'ssm_d'):
        p[name] = w[name][l][None, :]
    for name in ('w_in', 'w_gate', 'w_glu_val', 'w_glu_gate', 'w_attn_out', 'w_conv_out', 'w_mix_out', 'w_ffn_up',
                 'w_ffn_down'):
        p[name] = w[name][l]
        p[name + '_t'] = w[name][l].T
    p['conv_w'] = w['conv_w'][l]
    p['lr'] = w['ssm_a_re'][l].reshape(1, N_STATE)
    p['li'] = w['ssm_a_im'][l].reshape(1, N_STATE)
    p['ldt'] = jnp.repeat(w['ssm_log_dt'][l], SSM_STATE).reshape(1, N_STATE)
    p['b_re_t'] = w['ssm_b_re'][l].transpose(2, 0, 1).reshape(SSM_GROUP, N_STATE)
    p['b_im_t'] = w['ssm_b_im'][l].transpose(2, 0, 1).reshape(SSM_GROUP, N_STATE)
    p['ab_re'], p['ab_im'], bb_re_t, bb_im_t = _ssm_param_fwd(p['lr'], p['li'], p['ldt'], p['b_re_t'], p['b_im_t'])
    p['bb_re'] = _expand_bd(bb_re_t).astype(BF16)
    p['bb_im'] = _expand_bd(bb_im_t).astype(BF16)
    p['ct_re'] = _expand_bd(w['ssm_c_re'][l].transpose(1, 0, 2).reshape(SSM_GROUP, N_STATE)).astype(BF16)
    p['ct_im'] = _expand_bd(w['ssm_c_im'][l].transpose(1, 0, 2).reshape(SSM_GROUP, N_STATE)).astype(BF16)
    return p


def _layer_fwd(x, p):
    s = {'x': x}
    s['hb'], s['u'], q, k, v, s['cb'], s['cc'], s['cx'], s['gate'] = _fwd_in(
        x, p['norm_mix_pre'], p['w_in'], p['w_gate'], p['b_gate'])
    s['h_re'], s['h_im'], s['y_pre'] = _ssm_fwd(s['u'], p['bb_re'], p['bb_im'], p['ab_re'], p['ab_im'], p['ct_re'].T,
                                                p['ct_im'].T, p['ssm_d'])
    s['q'], s['k'], s['v'] = q, k, v
    o_r, s['r_tab'] = _attn_fwd(_heads_rows(q), _heads_blocks_t(k), _heads_blocks_r(v))
    s['o'] = o_r.transpose(1, 0, 2).reshape(x.shape[0], D_SB).astype(BF16)
    s['x1'], s['merged'], s['m2'] = _merge_fwd(
        s['y_pre'], s['o'], s['cb'], s['cc'], s['cx'], p['conv_w'], s['gate'], x, p['w_glu_val'], p['w_glu_gate'],
        p['w_attn_out'], p['w_conv_out'], p['w_mix_out'], p['norm_mix_post'])
    x2, s['h2'], s['f'] = _ffn_fwd(s['x1'], p['norm_ffn_pre'], p['norm_ffn_post'], p['w_ffn_up'], p['w_ffn_down'])
    return x2, s


def _layer_bwd(dx2, p, s):
    t = dx2.shape[0]
    g = {}
    dx1, da, r, df, g['norm_ffn_pre'], g['norm_ffn_post'] = _ffn_bwd(
        dx2, s['f'], s['h2'], s['x1'], p['norm_ffn_pre'], p['norm_ffn_post'], p['w_ffn_up'], p['w_ffn_up_t'],
        p['w_ffn_down_t'])
    g['w_ffn_up'] = _mm_tn(s['h2'], da)
    g['w_ffn_down'] = _mm_tn(r, df)

    (dgp, dy_pre, do, dcb, dyconv, dm2, d_val, d_glu, yg, dy_b, dy_c, cy, g['norm_mix_post'], g['b_gate']) = _merge_bwd(
        dx1, s['m2'], p['norm_mix_post'], p['w_mix_out_t'], s['gate'], s['y_pre'], s['o'], s['cb'], s['cc'], s['cx'],
        p['conv_w'], p['w_glu_val'], p['w_glu_gate'], p['w_attn_out'], p['w_conv_out'], p['w_glu_val_t'],
        p['w_glu_gate_t'], p['w_attn_out_t'], p['w_conv_out_t'])
    g['w_mix_out'] = _mm_tn(s['merged'], dm2)
    g['w_glu_val'] = _mm_tn(yg, d_val)
    g['w_glu_gate'] = _mm_tn(yg, d_glu)
    g['w_attn_out'] = _mm_tn(s['o'], dy_b)
    g['w_conv_out'] = _mm_tn(cy, dy_c)

    dcc, dcx, dcw = _conv_bwd(dyconv, s['cc'], s['cx'], p['conv_w'])
    g['conv_w'] = dcw[0:3]

    dq_r, dk_t, dv_t = _attn_bwd(_heads_rows(s['q']), _heads_blocks_t(s['q']), _heads_rows(do), _heads_blocks_t(do),
                                 _heads_blocks_t(s['k']), _heads_blocks_r(s['k']), _heads_blocks_t(s['v']), s['r_tab'])
    dq = dq_r.transpose(1, 0, 2).reshape(t, D_SB)
    dk = dk_t.transpose(1, 3, 0, 2).reshape(t, D_SB)
    dv = dv_t.transpose(1, 3, 0, 2).reshape(t, D_SB)

    du, dbb_re, dbb_im, dct_re, dct_im, dab_re, dab_im, g['ssm_d'] = _ssm_bwd(
        dy_pre, s['u'], s['h_re'], s['h_im'], p['ct_re'], p['ct_im'], p['bb_re'].T, p['bb_im'].T, p['ab_re'],
        p['ab_im'], p['ssm_d'])
    dlr, dli, dldt, db_re_t, db_im_t = _ssm_param_bwd(p['lr'], p['li'], p['ldt'], p['b_re_t'], p['b_im_t'], dab_re,
                                                      dab_im, _extract_bd(dbb_re), _extract_bd(dbb_im))
    g['ssm_a_re'] = dlr.reshape(SSM_GROUPS, SSM_STATE)
    g['ssm_a_im'] = dli.reshape(SSM_GROUPS, SSM_STATE)
    g['ssm_log_dt'] = dldt[0, :SSM_GROUPS]
    g['ssm_b_re'] = db_re_t.reshape(SSM_GROUP, SSM_GROUPS, SSM_STATE).transpose(1, 2, 0)
    g['ssm_b_im'] = db_im_t.reshape(SSM_GROUP, SSM_GROUPS, SSM_STATE).transpose(1, 2, 0)
    g['ssm_c_re'] = _extract_bd(dct_re).reshape(SSM_GROUP, SSM_GROUPS, SSM_STATE).transpose(1, 0, 2)
    g['ssm_c_im'] = _extract_bd(dct_im).reshape(SSM_GROUP, SSM_GROUPS, SSM_STATE).transpose(1, 0, 2)

    dp = jnp.concatenate([du.astype(BF16), dq.astype(BF16), dk.astype(BF16), dv.astype(BF16), dcb.astype(BF16),
                          dcc.astype(BF16), dcx.astype(BF16)], axis=1)
    dx, g['norm_mix_pre'] = _in_bwd(dp, dgp, p['w_in_t'], p['w_gate_t'], s['x'], p['norm_mix_pre'], dx1)
    g['w_in'] = _mm_tn(s['hb'], dp)
    g['w_gate'] = _mm_tn(s['hb'], dgp)
    for name in ('norm_mix_pre', 'norm_mix_post', 'b_gate', 'norm_ffn_pre', 'norm_ffn_post', 'ssm_d'):
        g[name] = g[name][0]
    return dx, g


def _local_step(x, target, w):
    ps, saved = [], []
    for l in range(DEPTH):
        ps.append(_prep_layer(w, l))
        x, s = _layer_fwd(x, ps[l])
        saved.append(s)
    dx, loss_blk = _loss_grad(x, target)
    grads = [None] * DEPTH
    for l in reversed(range(DEPTH)):
        dx, grads[l] = _layer_bwd(dx, ps[l], saved[l])
    return loss_blk, dx, {n: jnp.stack([grads[l][n] for l in range(DEPTH)]) for n in WEIGHTS}


def kernel(x, norm_mix_pre, norm_mix_post, w_in, w_gate, b_gate, ssm_a_re, ssm_a_im, ssm_log_dt, ssm_b_re,
           ssm_b_im, ssm_c_re, ssm_c_im, ssm_d, w_glu_val, w_glu_gate, w_attn_out, conv_w, w_conv_out,
           w_mix_out, norm_ffn_pre, norm_ffn_post, w_ffn_up, w_ffn_down, loss_target, m_norm_mix_pre,
           m_norm_mix_post, m_w_in, m_w_gate, m_b_gate, m_ssm_a_re, m_ssm_a_im, m_ssm_log_dt, m_ssm_b_re,
           m_ssm_b_im, m_ssm_c_re, m_ssm_c_im, m_ssm_d, m_w_glu_val, m_w_glu_gate, m_w_attn_out, m_conv_w,
           m_w_conv_out, m_w_mix_out, m_norm_ffn_pre, m_norm_ffn_post, m_w_ffn_up, m_w_ffn_down,
           v_norm_mix_pre, v_norm_mix_post, v_w_in, v_w_gate, v_b_gate, v_ssm_a_re, v_ssm_a_im, v_ssm_log_dt,
           v_ssm_b_re, v_ssm_b_im, v_ssm_c_re, v_ssm_c_im, v_ssm_d, v_w_glu_val, v_w_glu_gate, v_w_attn_out,
           v_conv_w, v_w_conv_out, v_w_mix_out, v_norm_ffn_pre, v_norm_ffn_post, v_w_ffn_up, v_w_ffn_down):
    given = dict(locals())
    shard = {n: given[n] for n in WEIGHTS}
    big = list(SHARDED)

    sent = [lax.bitcast_convert_type(shard[n], BF16) if n == 'conv_w' else shard[n].astype(BF16) for n in big]
    gathered = _gather_chips(_pack(sent, BF16))
    full = {n: shard[n] for n in REPLICATED}
    per_chip = [_unpack(gathered[j], [a.shape for a in sent]) for j in range(N_CHIPS)]
    for i, n in enumerate(big):
        parts = [per_chip[j][i] for j in range(N_CHIPS)]
        if n == 'conv_w':
            parts = [lax.bitcast_convert_type(a, F32) for a in parts]
        full[n] = jnp.concatenate(parts, axis=SHARDED[n])

    loss_blk, dx, grads = _local_step(x[0], loss_target[0], full)
    loss = lax.psum(loss_blk[0, 0], ("x", "y", "c"))

    to_chip = [_pack([jnp.split(grads[n], N_CHIPS, axis=SHARDED[n])[j] for n in big], BF16) for j in range(N_CHIPS)]
    core_sum = _sum_parts(_scatter_chips(jnp.stack(to_chip)))
    out_big = _adamw([core_sum, _swap_cores(core_sum)], _pack([shard[n] for n in big], F32),
                     _pack([given['m_' + n] for n in big], F32), _pack([given['v_' + n] for n in big], F32))
    all_parts = _gather_all(_pack([grads[n] for n in REPLICATED], F32))
    out_small = _adamw([all_parts[k] for k in range(N_DEV)], _pack([shard[n] for n in REPLICATED], F32),
                       _pack([given['m_' + n] for n in REPLICATED], F32),
                       _pack([given['v_' + n] for n in REPLICATED], F32))

    results = []
    for kind in range(4):
        got = dict(zip(big, _unpack(out_big[kind], [shard[n].shape for n in big])))
        got.update(zip(REPLICATED, _unpack(out_small[kind], [shard[n].shape for n in REPLICATED])))
        results += [got[n] for n in WEIGHTS]
    return (loss, dx[None], *results)
```

```python
import functools

import jax
import jax.numpy as jnp
from jax import lax
from jax.experimental import pallas as pl
from jax.experimental.pallas import tpu as pltpu

F32, BF16 = jnp.float32, jnp.bfloat16

D_MODEL = 1024
DEPTH = 2
SSM_GROUPS, SSM_GROUP, SSM_STATE = 16, 16, 64
D_SSM = SSM_GROUPS * SSM_GROUP
N_STATE = SSM_GROUPS * SSM_STATE
SB_HEADS, SB_HEAD_DIM = 8, 64
D_SB = SB_HEADS * SB_HEAD_DIM
D_CONV = 256
D_IN = D_SSM + 3 * D_SB + 3 * D_CONV
D_GATE = 3 * D_MODEL
D_FF = 4 * D_MODEL
EPS = 1e-6
SB_SCALE = SB_HEAD_DIM ** -0.5
GELU_C = 0.7978845608028654
GELU_A = 0.044715

ADAM_LR, ADAM_B1, ADAM_B2, ADAM_EPS, ADAM_WD, ADAM_STEP = 0.001, 0.9, 0.999, 1e-08, 0.01, 10

TM = 256
BQ = 256
SB_UNROLL = 4
R_LANES = 128
HALO = 8
VMEM_LIMIT_MB = 56

MESH = pl.DeviceIdType.MESH
WEIGHTS = ['norm_mix_pre', 'norm_mix_post', 'w_in', 'w_gate', 'b_gate', 'ssm_a_re', 'ssm_a_im', 'ssm_log_dt',
           'ssm_b_re', 'ssm_b_im', 'ssm_c_re', 'ssm_c_im', 'ssm_d', 'w_glu_val', 'w_glu_gate', 'w_attn_out', 'conv_w',
           'w_conv_out', 'w_mix_out', 'norm_ffn_pre', 'norm_ffn_post', 'w_ffn_up', 'w_ffn_down']
SHARDED = {'w_in': 2, 'w_gate': 2, 'w_glu_val': 2, 'w_glu_gate': 2, 'w_attn_out': 2, 'conv_w': 2, 'w_conv_out': 2,
           'w_mix_out': 1, 'w_ffn_up': 2, 'w_ffn_down': 1}
REPLICATED = [n for n in WEIGHTS if n not in SHARDED]
N_CHIPS = 4
N_DEV = 8


def _dot(a, b):
    return jnp.dot(a, b, preferred_element_type=F32)


def _dot_tn(a, b):
    return lax.dot_general(a, b, (((0,), (0,)), ((), ())), preferred_element_type=F32)


def _sigmoid(x):
    return 1.0 / (1.0 + jnp.exp(-x))


def _params(sem, vmem_mb=VMEM_LIMIT_MB):
    return pltpu.CompilerParams(dimension_semantics=sem, vmem_limit_bytes=vmem_mb << 20)


def _rows(tm, n):
    return pl.BlockSpec((tm, n), lambda i: (i, 0))


def _whole(shape):
    zeros = (0,) * len(shape)
    return pl.BlockSpec(shape, lambda *_: zeros)


def _sds(shape, dtype):
    return jax.ShapeDtypeStruct(shape, dtype)


def _rms_fwd(x, g):
    r = lax.rsqrt(jnp.mean(x * x, axis=-1, keepdims=True) + EPS)
    return x * r * g


def _rms_bwd(x, g, dy):
    r = lax.rsqrt(jnp.mean(x * x, axis=-1, keepdims=True) + EPS)
    xh = x * r
    dxh = dy * g
    dx = r * (dxh - xh * jnp.mean(dxh * xh, axis=-1, keepdims=True))
    return dx, dy * xh


def _colsum(a):
    return jnp.sum(a, axis=0, keepdims=True)


def _gelu(y):
    return 0.5 * y * (1.0 + jnp.tanh(GELU_C * (y + GELU_A * y * y * y)))


def _gelu_grad(y):
    th = jnp.tanh(GELU_C * (y + GELU_A * y * y * y))
    return 0.5 * (1.0 + th) + 0.5 * y * (1.0 - th * th) * GELU_C * (1.0 + 3.0 * GELU_A * y * y)


def _fwd_in(x, g_pre, w_in, w_gate, b_gate):
    t = x.shape[0]

    def body(x_ref, g_ref, win_ref, wg_ref, bg_ref, hb_ref, u_ref, q_ref, k_ref, v_ref, cb_ref, cc_ref, cx_ref, gate_ref):
        hb = _rms_fwd(x_ref[...], g_ref[...]).astype(BF16)
        hb_ref[...] = hb
        p = _dot(hb, win_ref[...])
        o = 0
        u_ref[...] = p[:, o:o + D_SSM]
        o += D_SSM
        q_ref[...] = (p[:, o:o + D_SB] * -SB_SCALE).astype(BF16)
        o += D_SB
        k_ref[...] = p[:, o:o + D_SB].astype(BF16)
        o += D_SB
        v_ref[...] = p[:, o:o + D_SB].astype(BF16)
        o += D_SB
        cb_ref[...] = p[:, o:o + D_CONV]
        o += D_CONV
        cc_ref[...] = p[:, o:o + D_CONV]
        o += D_CONV
        cx_ref[...] = p[:, o:o + D_CONV]
        gate_ref[...] = _sigmoid(_dot(hb, wg_ref[...]) + bg_ref[...])

    outs = (_sds((t, D_MODEL), BF16), _sds((t, D_SSM), F32), _sds((t, D_SB), BF16), _sds((t, D_SB), BF16),
            _sds((t, D_SB), BF16), _sds((t, D_CONV), F32), _sds((t, D_CONV), F32), _sds((t, D_CONV), F32),
            _sds((t, D_GATE), F32))
    return pl.pallas_call(
        body, name="fwd_in", grid=(t // TM,),
        in_specs=[_rows(TM, D_MODEL), _whole((1, D_MODEL)), _whole((D_MODEL, D_IN)), _whole((D_MODEL, D_GATE)),
                  _whole((1, D_GATE))],
        out_specs=tuple(_rows(TM, s.shape[1]) for s in outs), out_shape=outs,
        compiler_params=_params(("parallel",)),
    )(x, g_pre, w_in, w_gate, b_gate)


def _ssm_scan_rows(n_rows, ar, ai, sign, in_re_ref, in_im_ref, out_re_ref, out_im_ref, carry_re_ref, carry_im_ref,
                   reverse):
    def group(gi, carry):
        hr, hi = carry
        g = (n_rows // 8 - 1 - gi) if reverse else gi
        r0 = pl.multiple_of(g * 8, 8)
        cr = in_re_ref[pl.ds(r0, 8), :]
        ci = in_im_ref[pl.ds(r0, 8), :]
        outs_r, outs_i = [None] * 8, [None] * 8
        for kk in range(8):
            k = 7 - kk if reverse else kk
            nr = ar * hr - sign * ai * hi + cr[k:k + 1, :]
            ni = ar * hi + sign * ai * hr + ci[k:k + 1, :]
            hr, hi = nr, ni
            outs_r[k], outs_i[k] = hr, hi
        out_re_ref[pl.ds(r0, 8), :] = jnp.concatenate(outs_r, axis=0)
        out_im_ref[pl.ds(r0, 8), :] = jnp.concatenate(outs_i, axis=0)
        return hr, hi

    hr, hi = lax.fori_loop(0, n_rows // 8, group, (carry_re_ref[...], carry_im_ref[...]))
    carry_re_ref[...] = hr
    carry_im_ref[...] = hi


def _ssm_fwd(u, bb_re, bb_im, ab_re, ab_im, c_re, c_im, d_skip):
    t = u.shape[0]

    def body(u_ref, bre_ref, bim_ref, ar_ref, ai_ref, cre_ref, cim_ref, d_ref, hre_ref, him_ref, y_ref,
             bur_ref, bui_ref, car_ref, cai_ref):
        @pl.when(pl.program_id(0) == 0)
        def _():
            car_ref[...] = jnp.zeros_like(car_ref)
            cai_ref[...] = jnp.zeros_like(cai_ref)

        uv = u_ref[...]
        ub = uv.astype(BF16)
        bur_ref[...] = _dot(ub, bre_ref[...])
        bui_ref[...] = _dot(ub, bim_ref[...])
        _ssm_scan_rows(TM, ar_ref[...], ai_ref[...], 1.0, bur_ref, bui_ref, hre_ref, him_ref, car_ref, cai_ref, False)
        y_ref[...] = (_dot(hre_ref[...].astype(BF16), cre_ref[...]) - _dot(him_ref[...].astype(BF16), cim_ref[...])
                      + d_ref[...] * uv)

    outs = (_sds((t, N_STATE), F32), _sds((t, N_STATE), F32), _sds((t, D_SSM), F32))
    return pl.pallas_call(
        body, name="ssm_fwd", grid=(t // TM,),
        in_specs=[_rows(TM, D_SSM), _whole((D_SSM, N_STATE)), _whole((D_SSM, N_STATE)), _whole((1, N_STATE)),
                  _whole((1, N_STATE)), _whole((N_STATE, D_SSM)), _whole((N_STATE, D_SSM)), _whole((1, D_SSM))],
        out_specs=(_rows(TM, N_STATE), _rows(TM, N_STATE), _rows(TM, D_SSM)), out_shape=outs,
        scratch_shapes=[pltpu.VMEM((TM, N_STATE), F32), pltpu.VMEM((TM, N_STATE), F32),
                        pltpu.VMEM((1, N_STATE), F32), pltpu.VMEM((1, N_STATE), F32)],
        compiler_params=_params(("arbitrary",)),
    )(u, bb_re, bb_im, ab_re, ab_im, c_re, c_im, d_skip)


def _neg_abs(x):
    return lax.bitcast_convert_type(lax.bitcast_convert_type(x, jnp.int32) | jnp.int32(-2 ** 31), F32)


def _sb_log1m(nz, mask):
    b = jnp.minimum(nz, 0.0) - jnp.log(1.0 + jnp.exp(_neg_abs(nz)))
    return b if mask is None else jnp.where(mask, b, 0.0)


def _sb_weights(nz, b, incl, r_run, mask):
    w = jnp.exp((r_run + _dot(b.astype(BF16), incl)) - nz)
    return w if mask is None else jnp.where(mask, w, 0.0)


def _rowsum(a):
    return jnp.sum(a, axis=1, keepdims=True)


def _attn_fwd(q_r, k_t, v_r):
    h, t, _ = q_r.shape
    nk = t // BQ
    assert nk <= R_LANES

    def body(q_ref, kt_ref, v_ref, o_ref, rtab_ref):
        i = pl.program_id(1)
        q = q_ref[0]
        rows = lax.broadcasted_iota(jnp.int32, (BQ, BQ), 0)
        cols = lax.broadcasted_iota(jnp.int32, (BQ, BQ), 1)
        tri = rows > cols
        incl = (rows >= cols).astype(BF16)
        lane = lax.broadcasted_iota(jnp.int32, (BQ, R_LANES), 1)

        def group(js, carry, mask=None):
            r_run, acc, rtab = carry
            nzs = [_dot(q, kt_ref[0, j]) for j in js]
            bs = [_sb_log1m(nz, mask) for nz in nzs]
            rs = [r_run]
            for b in bs:
                rs.append(rs[-1] + _rowsum(b))
            ws = [_sb_weights(nz, b, incl, r, mask) for nz, b, r in zip(nzs, bs, rs)]
            for j, w, r in zip(js, ws, rs):
                acc = acc + _dot(w.astype(BF16), v_ref[0, j])
                rtab = jnp.where(lane == j, r, rtab)
            return rs[-1], acc, rtab

        carry = (jnp.zeros((BQ, 1), F32), jnp.zeros((BQ, SB_HEAD_DIM), F32), jnp.zeros((BQ, R_LANES), F32))
        carry = group([i], carry, tri)
        rem = i % SB_UNROLL
        carry = lax.fori_loop(0, rem, lambda n, c: group([i - 1 - n], c), carry)
        carry = lax.fori_loop(0, i // SB_UNROLL,
                              lambda m, c: group([i - 1 - rem - SB_UNROLL * m - k for k in range(SB_UNROLL)], c), carry)
        o_ref[0] = carry[1]
        rtab_ref[0] = carry[2]

    return pl.pallas_call(
        body, name="attn_fwd", grid=(h, nk),
        in_specs=[pl.BlockSpec((1, BQ, SB_HEAD_DIM), lambda hh, i: (hh, i, 0)),
                  pl.BlockSpec((1, nk, SB_HEAD_DIM, BQ), lambda hh, i: (hh, 0, 0, 0)),
                  pl.BlockSpec((1, nk, BQ, SB_HEAD_DIM), lambda hh, i: (hh, 0, 0, 0))],
        out_specs=(pl.BlockSpec((1, BQ, SB_HEAD_DIM), lambda hh, i: (hh, i, 0)),
                   pl.BlockSpec((1, BQ, R_LANES), lambda hh, i: (hh, i, 0))),
        out_shape=(_sds((h, t, SB_HEAD_DIM), F32), _sds((h, t, R_LANES), F32)),
        compiler_params=_params(("parallel", "parallel")),
    )(q_r, k_t, v_r)


def _merge_fwd(y_pre, o, cb, cc, cx, conv_w, gate, x, w_val, w_glu, w_ao, w_co, w_mo, g_post):
    t = x.shape[0]

    def body(y_ref, o_ref, cb_ref, cc_ref, cx_ref, ccp_ref, cxp_ref, cw_ref, gate_ref, x_ref, wv_ref, wg_ref, wao_ref,
             wco_ref, wmo_ref, gp_ref, x1_ref, mg_ref, m2_ref):
        i = pl.program_id(0)
        ygb = _gelu(y_ref[...]).astype(BF16)
        y_a = _dot(ygb, wv_ref[...]) * _sigmoid(_dot(ygb, wg_ref[...]))
        y_b = _dot(o_ref[...], wao_ref[...])
        yconv = _conv_fwd(cc_ref[...] * cx_ref[...], jnp.where(i > 0, ccp_ref[...] * cxp_ref[...], 0.0), cw_ref[...])[0]
        y_c = _dot((cb_ref[...] * yconv).astype(BF16), wco_ref[...])
        gate_v = gate_ref[...]
        merged = (gate_v[:, :D_MODEL] * y_a + gate_v[:, D_MODEL:2 * D_MODEL] * y_b
                  + gate_v[:, 2 * D_MODEL:] * y_c).astype(BF16)
        mg_ref[...] = merged
        m2 = _dot(merged, wmo_ref[...])
        m2_ref[...] = m2
        x1_ref[...] = x_ref[...] + _rms_fwd(m2, gp_ref[...])

    outs = (_sds((t, D_MODEL), F32), _sds((t, D_MODEL), BF16), _sds((t, D_MODEL), F32))
    return pl.pallas_call(
        body, name="merge_fwd", grid=(t // TM,),
        in_specs=[_rows(TM, D_SSM), _rows(TM, D_SB), _rows(TM, D_CONV), _rows(TM, D_CONV), _rows(TM, D_CONV),
                  _halo_before(D_CONV), _halo_before(D_CONV), _whole((3, D_CONV)), _rows(TM, D_GATE),
                  _rows(TM, D_MODEL), _whole((D_SSM, D_MODEL)), _whole((D_SSM, D_MODEL)), _whole((D_SB, D_MODEL)),
                  _whole((D_CONV, D_MODEL)), _whole((D_MODEL, D_MODEL)), _whole((1, D_MODEL))],
        out_specs=tuple(_rows(TM, D_MODEL) for _ in outs), out_shape=outs,
        compiler_params=_params(("parallel",)),
    )(y_pre, o, cb, cc, cx, cc, cx, conv_w, gate, x, w_val, w_glu, w_ao, w_co, w_mo, g_post)


def _halo_before(n):
    return pl.BlockSpec((HALO, n), lambda i: (jnp.maximum(i * (TM // HALO) - 1, 0), 0))


def _halo_after(n, n_tiles):
    return pl.BlockSpec((HALO, n), lambda i: (jnp.minimum((i + 1) * (TM // HALO), n_tiles * (TM // HALO) - 1), 0))


def _conv_fwd(z, z_before, cw):
    row = lax.broadcasted_iota(jnp.int32, z.shape, 0)
    z1 = jnp.where(row == 0, z_before[HALO - 1:HALO, :], pltpu.roll(z, 1, axis=0))
    z2 = jnp.where(row == 0, z_before[HALO - 2:HALO - 1, :],
                   jnp.where(row == 1, z_before[HALO - 1:HALO, :], pltpu.roll(z, 2, axis=0)))
    return cw[0:1, :] * z2 + cw[1:2, :] * z1 + cw[2:3, :] * z, z1, z2


def _ffn_fwd(x1, g1, g2, w_up, w_down):
    t = x1.shape[0]

    def body(x_ref, g1_ref, g2_ref, wu_ref, wd_ref, x2_ref, h2_ref, f_ref):
        xv = x_ref[...]
        h2 = _rms_fwd(xv, g1_ref[...]).astype(BF16)
        h2_ref[...] = h2
        ra = jnp.maximum(_dot(h2, wu_ref[...]), 0.0)
        f = _dot((ra * ra).astype(BF16), wd_ref[...])
        f_ref[...] = f
        x2_ref[...] = xv + _rms_fwd(f, g2_ref[...])

    outs = (_sds((t, D_MODEL), F32), _sds((t, D_MODEL), BF16), _sds((t, D_MODEL), F32))
    return pl.pallas_call(
        body, name="ffn_fwd", grid=(t // TM,),
        in_specs=[_rows(TM, D_MODEL), _whole((1, D_MODEL)), _whole((1, D_MODEL)), _whole((D_MODEL, D_FF)),
                  _whole((D_FF, D_MODEL))],
        out_specs=tuple(_rows(TM, D_MODEL) for _ in outs), out_shape=outs,
        compiler_params=_params(("parallel",)),
    )(x1, g1, g2, w_up, w_down)


def _loss_grad(y, target):
    t = y.shape[0]

    def body(y_ref, t_ref, dy_ref, loss_ref):
        @pl.when(pl.program_id(0) == 0)
        def _():
            loss_ref[...] = jnp.zeros_like(loss_ref)

        err = y_ref[...] - t_ref[...]
        dy_ref[...] = err * (1.0 / D_MODEL)
        loss_ref[...] += 0.5 * jnp.sum(jnp.mean(err * err, axis=-1, keepdims=True), axis=0, keepdims=True)

    return pl.pallas_call(
        body, name="loss_grad", grid=(t // TM,),
        in_specs=[_rows(TM, D_MODEL), _rows(TM, D_MODEL)],
        out_specs=(_rows(TM, D_MODEL), _whole((8, 128))),
        out_shape=(_sds((t, D_MODEL), F32), _sds((8, 128), F32)),
        compiler_params=_params(("arbitrary",)),
    )(y, target)


def _ffn_bwd(dx2, f, h2, x1, g1, g2, w_up, w_up_t, w_down_t):
    t = x1.shape[0]

    def body(dx2_ref, f_ref, h2_ref, x1_ref, g1_ref, g2_ref, wu_ref, wut_ref, wdt_ref, dx1_ref, da_ref, r_ref, df_ref,
             dg1_ref, dg2_ref):
        @pl.when(pl.program_id(0) == 0)
        def _():
            dg1_ref[...] = jnp.zeros_like(dg1_ref)
            dg2_ref[...] = jnp.zeros_like(dg2_ref)

        dx2 = dx2_ref[...]
        df, dg2 = _rms_bwd(f_ref[...], g2_ref[...], dx2)
        dg2_ref[...] += _colsum(dg2)
        dfb = df.astype(BF16)
        df_ref[...] = dfb
        ra = jnp.maximum(_dot(h2_ref[...], wu_ref[...]), 0.0)
        r_ref[...] = (ra * ra).astype(BF16)
        da = (_dot(dfb, wdt_ref[...]) * (2.0 * ra)).astype(BF16)
        da_ref[...] = da
        dxn, dg1 = _rms_bwd(x1_ref[...], g1_ref[...], _dot(da, wut_ref[...]))
        dg1_ref[...] += _colsum(dg1)
        dx1_ref[...] = dx2 + dxn

    outs = (_sds((t, D_MODEL), F32), _sds((t, D_FF), BF16), _sds((t, D_FF), BF16), _sds((t, D_MODEL), BF16),
            _sds((1, D_MODEL), F32), _sds((1, D_MODEL), F32))
    return pl.pallas_call(
        body, name="ffn_bwd", grid=(t // TM,),
        in_specs=[_rows(TM, D_MODEL), _rows(TM, D_MODEL), _rows(TM, D_MODEL), _rows(TM, D_MODEL), _whole((1, D_MODEL)),
                  _whole((1, D_MODEL)), _whole((D_MODEL, D_FF)), _whole((D_FF, D_MODEL)), _whole((D_MODEL, D_FF))],
        out_specs=(_rows(TM, D_MODEL), _rows(TM, D_FF), _rows(TM, D_FF), _rows(TM, D_MODEL), _whole((1, D_MODEL)),
                   _whole((1, D_MODEL))),
        out_shape=outs, compiler_params=_params(("arbitrary",), 60),
    )(dx2, f, h2, x1, g1, g2, w_up, w_up_t, w_down_t)


def _mm_tn(a, b):
    t, k = a.shape
    n = b.shape[1]
    tk, tt = min(k, 1024), min(t, 1024)
    tn = next(c for c in (1024, 512, 256) if n % c == 0)

    def body(a_ref, b_ref, o_ref):
        @pl.when(pl.program_id(2) == 0)
        def _():
            o_ref[...] = jnp.zeros_like(o_ref)

        o_ref[...] += _dot_tn(a_ref[...], b_ref[...])

    return pl.pallas_call(
        body, name="mm_tn", grid=(k // tk, n // tn, t // tt),
        in_specs=[pl.BlockSpec((tt, tk), lambda i, j, s: (s, i)), pl.BlockSpec((tt, tn), lambda i, j, s: (s, j))],
        out_specs=pl.BlockSpec((tk, tn), lambda i, j, s: (i, j)), out_shape=_sds((k, n), F32),
        compiler_params=_params(("parallel", "parallel", "arbitrary")),
    )(a, b)


def _merge_bwd(dx1, m2, g_post, w_mo_t, gate, y_pre, o, cb, cc, cx, conv_w, w_val, w_glu, w_ao, w_co, w_val_t, w_glu_t,
               w_ao_t, w_co_t):
    t = dx1.shape[0]

    def body(dx1_ref, m2_ref, gp_ref, wmot_ref, gate_ref, y_ref, o_ref, cb_ref, cc_ref, cx_ref, ccp_ref, cxp_ref, cw_ref,
             wv_ref, wg_ref, wao_ref, wco_ref, wvt_ref, wgt_ref, waot_ref, wcot_ref,
             dgp_ref, dyp_ref, do_ref, dcb_ref, dyc_ref, dm2_ref, da_ref, dbg_ref, yg_ref, dyb_ref, dycc_ref, cy_ref,
             dgpost_ref, dbgate_ref):
        i = pl.program_id(0)

        @pl.when(i == 0)
        def _():
            dgpost_ref[...] = jnp.zeros_like(dgpost_ref)
            dbgate_ref[...] = jnp.zeros_like(dbgate_ref)

        dm2, dgpost = _rms_bwd(m2_ref[...], gp_ref[...], dx1_ref[...])
        dgpost_ref[...] += _colsum(dgpost)
        dm2b = dm2.astype(BF16)
        dm2_ref[...] = dm2b
        dmerged = _dot(dm2b, wmot_ref[...])

        yv = y_ref[...]
        ygb = _gelu(yv).astype(BF16)
        yg_ref[...] = ygb
        a_val = _dot(ygb, wv_ref[...])
        s_glu = _sigmoid(_dot(ygb, wg_ref[...]))
        y_a = a_val * s_glu
        y_b = _dot(o_ref[...], wao_ref[...])
        cbv = cb_ref[...]
        yconv = _conv_fwd(cc_ref[...] * cx_ref[...], jnp.where(i > 0, ccp_ref[...] * cxp_ref[...], 0.0), cw_ref[...])[0]
        cyb = (cbv * yconv).astype(BF16)
        cy_ref[...] = cyb
        y_c = _dot(cyb, wco_ref[...])

        gate_v = gate_ref[...]
        g_a, g_b, g_c = gate_v[:, :D_MODEL], gate_v[:, D_MODEL:2 * D_MODEL], gate_v[:, 2 * D_MODEL:]
        dgp = jnp.concatenate([dmerged * y_a * g_a * (1.0 - g_a), dmerged * y_b * g_b * (1.0 - g_b),
                               dmerged * y_c * g_c * (1.0 - g_c)], axis=1)
        dbgate_ref[...] += _colsum(dgp)
        dgp_ref[...] = dgp.astype(BF16)

        dy_a = dmerged * g_a
        d_val = (dy_a * s_glu).astype(BF16)
        d_glu = (dy_a * a_val * s_glu * (1.0 - s_glu)).astype(BF16)
        da_ref[...] = d_val
        dbg_ref[...] = d_glu
        dyp_ref[...] = (_dot(d_val, wvt_ref[...]) + _dot(d_glu, wgt_ref[...])) * _gelu_grad(yv)

        dy_b = (dmerged * g_b).astype(BF16)
        dyb_ref[...] = dy_b
        do_ref[...] = _dot(dy_b, waot_ref[...]).astype(BF16)

        dy_c = (dmerged * g_c).astype(BF16)
        dycc_ref[...] = dy_c
        dcy = _dot(dy_c, wcot_ref[...])
        dcb_ref[...] = dcy * yconv
        dyc_ref[...] = dcy * cbv

    outs = (_sds((t, D_GATE), BF16), _sds((t, D_SSM), F32), _sds((t, D_SB), BF16), _sds((t, D_CONV), F32),
            _sds((t, D_CONV), F32), _sds((t, D_MODEL), BF16), _sds((t, D_MODEL), BF16), _sds((t, D_MODEL), BF16),
            _sds((t, D_SSM), BF16), _sds((t, D_MODEL), BF16), _sds((t, D_MODEL), BF16), _sds((t, D_CONV), BF16),
            _sds((1, D_MODEL), F32), _sds((1, D_GATE), F32))
    out_specs = tuple(_rows(TM, s.shape[1]) for s in outs[:-2]) + (_whole((1, D_MODEL)), _whole((1, D_GATE)))
    return pl.pallas_call(
        body, name="merge_bwd", grid=(t // TM,),
        in_specs=[_rows(TM, D_MODEL), _rows(TM, D_MODEL), _whole((1, D_MODEL)), _whole((D_MODEL, D_MODEL)),
                  _rows(TM, D_GATE), _rows(TM, D_SSM), _rows(TM, D_SB), _rows(TM, D_CONV), _rows(TM, D_CONV),
                  _rows(TM, D_CONV), _halo_before(D_CONV), _halo_before(D_CONV), _whole((3, D_CONV)),
                  _whole((D_SSM, D_MODEL)), _whole((D_SSM, D_MODEL)), _whole((D_SB, D_MODEL)),
                  _whole((D_CONV, D_MODEL)), _whole((D_MODEL, D_SSM)), _whole((D_MODEL, D_SSM)),
                  _whole((D_MODEL, D_SB)), _whole((D_MODEL, D_CONV))],
        out_specs=out_specs, out_shape=outs, compiler_params=_params(("arbitrary",)),
    )(dx1, m2, g_post, w_mo_t, gate, y_pre, o, cb, cc, cx, cc, cx, conv_w, w_val, w_glu, w_ao, w_co, w_val_t, w_glu_t,
      w_ao_t, w_co_t)


def _conv_bwd(dyconv, cc, cx, conv_w):
    t = dyconv.shape[0]
    nt = t // TM

    def body(dy_ref, dya_ref, cc_ref, cx_ref, ccp_ref, cxp_ref, cw_ref, dcc_ref, dcx_ref, dcw_ref):
        i = pl.program_id(0)

        @pl.when(i == 0)
        def _():
            dcw_ref[...] = jnp.zeros_like(dcw_ref)

        dy = dy_ref[...]
        dy_after = jnp.where(i < nt - 1, dya_ref[...], 0.0)
        row = lax.broadcasted_iota(jnp.int32, dy.shape, 0)
        dy1 = jnp.where(row == TM - 1, dy_after[0:1, :], pltpu.roll(dy, TM - 1, axis=0))
        dy2 = jnp.where(row == TM - 1, dy_after[1:2, :],
                        jnp.where(row == TM - 2, dy_after[0:1, :], pltpu.roll(dy, TM - 2, axis=0)))
        cw = cw_ref[...]
        dz = cw[2:3, :] * dy + cw[1:2, :] * dy1 + cw[0:1, :] * dy2
        ccv, cxv = cc_ref[...], cx_ref[...]
        dcc_ref[...] = dz * cxv
        dcx_ref[...] = dz * ccv
        z = ccv * cxv
        _, z1, z2 = _conv_fwd(z, jnp.where(i > 0, ccp_ref[...] * cxp_ref[...], 0.0), cw)
        dcw_ref[0:1, :] += _colsum(dy * z2)
        dcw_ref[1:2, :] += _colsum(dy * z1)
        dcw_ref[2:3, :] += _colsum(dy * z)

    return pl.pallas_call(
        body, name="conv_bwd", grid=(nt,),
        in_specs=[_rows(TM, D_CONV), _halo_after(D_CONV, nt), _rows(TM, D_CONV), _rows(TM, D_CONV),
                  _halo_before(D_CONV), _halo_before(D_CONV), _whole((3, D_CONV))],
        out_specs=(_rows(TM, D_CONV), _rows(TM, D_CONV), _whole((8, D_CONV))),
        out_shape=(_sds((t, D_CONV), F32), _sds((t, D_CONV), F32), _sds((8, D_CONV), F32)),
        compiler_params=_params(("arbitrary",)),
    )(dyconv, dyconv, cc, cx, cc, cx, conv_w)


def _attn_bwd(q_r, q_t, do_r, do_t, k_t, k_r, v_t, r_tab):
    h, t, _ = q_r.shape
    nk = t // BQ

    def body(q_ref, qt_ref, do_ref, dot_ref, kt_ref, kr_ref, vt_ref, rtab_ref, dq_ref, dkt_ref, dvt_ref):
        i = pl.program_id(1)

        @pl.when(i == 0)
        def _():
            dkt_ref[...] = jnp.zeros_like(dkt_ref)
            dvt_ref[...] = jnp.zeros_like(dvt_ref)

        q, q_tr, do, do_tr, rtab = q_ref[0], qt_ref[0, 0], do_ref[0], dot_ref[0, 0], rtab_ref[0]
        rows = lax.broadcasted_iota(jnp.int32, (BQ, BQ), 0)
        cols = lax.broadcasted_iota(jnp.int32, (BQ, BQ), 1)
        tri = rows > cols
        incl = (rows >= cols).astype(BF16)
        lower = (rows < cols).astype(BF16)
        lane = lax.broadcasted_iota(jnp.int32, (BQ, R_LANES), 1)

        def scores(j, mask):
            nz = _dot(q, kt_ref[0, j])
            b = _sb_log1m(nz, mask)
            w = _sb_weights(nz, b, incl, _rowsum(jnp.where(lane == j, rtab, 0.0)), mask)
            return b, w, w * _dot(do, vt_ref[0, j])

        def finish(j, b, w, e, p_run, dq, mask):
            p = p_run + _dot(e.astype(BF16), lower)
            dz = (e + p) * jnp.exp(b) - p
            if mask is not None:
                dz = jnp.where(mask, dz, 0.0)
            dzb = dz.astype(BF16)
            dvt_ref[0, j] += _dot(do_tr, w.astype(BF16))
            dkt_ref[0, j] -= _dot(q_tr, dzb)
            return dq + _dot(dzb, kr_ref[0, j])

        def group(js, carry, mask=None):
            p_run, dq = carry
            parts = [scores(j, mask) for j in js]
            ps = [p_run]
            for _, _, e in parts:
                ps.append(ps[-1] + _rowsum(e))
            for j, (b, w, e), p in zip(js, parts, ps):
                dq = finish(j, b, w, e, p, dq, mask)
            return ps[-1], dq

        carry = (jnp.zeros((BQ, 1), F32), jnp.zeros((BQ, SB_HEAD_DIM), F32))
        full = i // SB_UNROLL
        carry = lax.fori_loop(0, full, lambda m, c: group([SB_UNROLL * m + k for k in range(SB_UNROLL)], c), carry)
        carry = lax.fori_loop(0, i % SB_UNROLL, lambda n, c: group([SB_UNROLL * full + n], c), carry)
        dq_ref[0] = group([i], carry, tri)[1] * SB_SCALE

    row_blk = pl.BlockSpec((1, BQ, SB_HEAD_DIM), lambda hh, i: (hh, i, 0))
    col_blk = pl.BlockSpec((1, 1, SB_HEAD_DIM, BQ), lambda hh, i: (hh, i, 0, 0))
    head_t = pl.BlockSpec((1, nk, SB_HEAD_DIM, BQ), lambda hh, i: (hh, 0, 0, 0))
    head_r = pl.BlockSpec((1, nk, BQ, SB_HEAD_DIM), lambda hh, i: (hh, 0, 0, 0))
    return pl.pallas_call(
        body, name="attn_bwd", grid=(h, nk),
        in_specs=[row_blk, col_blk, row_blk, col_blk, head_t, head_r, head_t,
                  pl.BlockSpec((1, BQ, R_LANES), lambda hh, i: (hh, i, 0))],
        out_specs=(row_blk, head_t, head_t),
        out_shape=(_sds((h, t, SB_HEAD_DIM), F32), _sds((h, nk, SB_HEAD_DIM, BQ), F32),
                   _sds((h, nk, SB_HEAD_DIM, BQ), F32)),
        compiler_params=_params(("parallel", "arbitrary")),
    )(q_r, q_t, do_r, do_t, k_t, k_r, v_t, r_tab)


def _ssm_bwd(dy, u, h_re, h_im, ct_re, ct_im, bt_re, bt_im, ab_re, ab_im, d_skip):
    t = u.shape[0]
    nt = t // TM

    def rev(n):
        return pl.BlockSpec((TM, n), lambda i: (nt - 1 - i, 0))

    def before(n):
        return pl.BlockSpec((HALO, n), lambda i: (jnp.maximum((nt - 1 - i) * (TM // HALO) - 1, 0), 0))

    def body(dy_ref, u_ref, hre_ref, him_ref, hrp_ref, hip_ref, ctre_ref, ctim_ref, btre_ref, btim_ref, ar_ref, ai_ref,
             d_ref, du_ref, dbre_ref, dbim_ref, dcre_ref, dcim_ref, dar_ref, dai_ref, dd_ref,
             ghr_ref, ghi_ref, lamr_ref, lami_ref, car_ref, cai_ref):
        i = pl.program_id(0)

        @pl.when(i == 0)
        def _():
            for ref in (car_ref, cai_ref, dbre_ref, dbim_ref, dcre_ref, dcim_ref, dar_ref, dai_ref, dd_ref):
                ref[...] = jnp.zeros_like(ref)

        dyv = dy_ref[...]
        dyb = dyv.astype(BF16)
        ghr_ref[...] = _dot(dyb, ctre_ref[...])
        ghi_ref[...] = -_dot(dyb, ctim_ref[...])
        _ssm_scan_rows(TM, ar_ref[...], ai_ref[...], -1.0, ghr_ref, ghi_ref, lamr_ref, lami_ref, car_ref, cai_ref, True)
        lam_r, lam_i = lamr_ref[...], lami_ref[...]
        lam_rb, lam_ib = lam_r.astype(BF16), lam_i.astype(BF16)
        uv = u_ref[...]
        ub = uv.astype(BF16)
        du_ref[...] = _dot(lam_rb, btre_ref[...]) + _dot(lam_ib, btim_ref[...]) + d_ref[...] * dyv
        dbre_ref[...] += _dot_tn(ub, lam_rb)
        dbim_ref[...] += _dot_tn(ub, lam_ib)
        h_r, h_i = hre_ref[...], him_ref[...]
        dcre_ref[...] += _dot_tn(dyb, h_r.astype(BF16))
        dcim_ref[...] -= _dot_tn(dyb, h_i.astype(BF16))
        first = i == nt - 1
        row = lax.broadcasted_iota(jnp.int32, h_r.shape, 0)
        h_r1 = jnp.where(row == 0, jnp.where(first, 0.0, hrp_ref[HALO - 1:HALO, :]), pltpu.roll(h_r, 1, axis=0))
        h_i1 = jnp.where(row == 0, jnp.where(first, 0.0, hip_ref[HALO - 1:HALO, :]), pltpu.roll(h_i, 1, axis=0))
        dar_ref[...] += _colsum(lam_r * h_r1 + lam_i * h_i1)
        dai_ref[...] += _colsum(lam_i * h_r1 - lam_r * h_i1)
        dd_ref[...] += _colsum(dyv * uv)

    outs = (_sds((t, D_SSM), F32), _sds((D_SSM, N_STATE), F32), _sds((D_SSM, N_STATE), F32),
            _sds((D_SSM, N_STATE), F32), _sds((D_SSM, N_STATE), F32), _sds((1, N_STATE), F32),
            _sds((1, N_STATE), F32), _sds((1, D_SSM), F32))
    return pl.pallas_call(
        body, name="ssm_bwd", grid=(nt,),
        in_specs=[rev(D_SSM), rev(D_SSM), rev(N_STATE), rev(N_STATE), before(N_STATE), before(N_STATE),
                  _whole((D_SSM, N_STATE)), _whole((D_SSM, N_STATE)), _whole((N_STATE, D_SSM)),
                  _whole((N_STATE, D_SSM)), _whole((1, N_STATE)), _whole((1, N_STATE)), _whole((1, D_SSM))],
        out_specs=(rev(D_SSM),) + tuple(_whole(s.shape) for s in outs[1:]), out_shape=outs,
        scratch_shapes=[pltpu.VMEM((TM, N_STATE), F32) for _ in range(4)]
        + [pltpu.VMEM((1, N_STATE), F32), pltpu.VMEM((1, N_STATE), F32)],
        compiler_params=_params(("arbitrary",)),
    )(dy, u, h_re, h_im, h_re, h_im, ct_re, ct_im, bt_re, bt_im, ab_re, ab_im, d_skip)


def _in_bwd(dp, dgp, w_in_t, w_gate_t, x, g_pre, dx_res):
    t = x.shape[0]

    def body(dp_ref, dgp_ref, wit_ref, wgt_ref, x_ref, g_ref, dxr_ref, dx_ref, dg_ref):
        @pl.when(pl.program_id(0) == 0)
        def _():
            dg_ref[...] = jnp.zeros_like(dg_ref)

        dh = _dot(dp_ref[...], wit_ref[...]) + _dot(dgp_ref[...], wgt_ref[...])
        dxn, dg = _rms_bwd(x_ref[...], g_ref[...], dh)
        dg_ref[...] += _colsum(dg)
        dx_ref[...] = dxr_ref[...] + dxn

    return pl.pallas_call(
        body, name="in_bwd", grid=(t // TM,),
        in_specs=[_rows(TM, D_IN), _rows(TM, D_GATE), _whole((D_IN, D_MODEL)), _whole((D_GATE, D_MODEL)),
                  _rows(TM, D_MODEL), _whole((1, D_MODEL)), _rows(TM, D_MODEL)],
        out_specs=(_rows(TM, D_MODEL), _whole((1, D_MODEL))),
        out_shape=(_sds((t, D_MODEL), F32), _sds((1, D_MODEL), F32)),
        compiler_params=_params(("arbitrary",)),
    )(dp, dgp, w_in_t, w_gate_t, x, g_pre, dx_res)


def _ssm_disc(lr, li, ldt):
    dt = jnp.exp(ldt)
    mag = jnp.exp(lr * dt)
    th = li * dt
    cs, sn = jnp.cos(th), jnp.sin(th)
    ab_re, ab_im = mag * cs, mag * sn
    den = lr * lr + li * li
    xr = ab_re - 1.0
    co_re = (xr * lr + ab_im * li) / den
    co_im = (ab_im * lr - xr * li) / den
    return dt, mag, cs, sn, ab_re, ab_im, den, xr, co_re, co_im


def _ssm_param_fwd(lr, li, ldt, b_re_t, b_im_t):
    def body(lr_ref, li_ref, ldt_ref, br_ref, bi_ref, are_ref, aim_ref, bbr_ref, bbi_ref):
        _, _, _, _, ab_re, ab_im, _, _, co_re, co_im = _ssm_disc(lr_ref[...], li_ref[...], ldt_ref[...])
        are_ref[...] = ab_re
        aim_ref[...] = ab_im
        br, bi = br_ref[...], bi_ref[...]
        bbr_ref[...] = co_re * br - co_im * bi
        bbi_ref[...] = co_re * bi + co_im * br

    vec, mat = _sds((1, N_STATE), F32), _sds((SSM_GROUP, N_STATE), F32)
    return pl.pallas_call(body, name="ssm_param_fwd", out_shape=(vec, vec, mat, mat))(lr, li, ldt, b_re_t, b_im_t)


def _ssm_param_bwd(lr, li, ldt, b_re_t, b_im_t, dab_re, dab_im, dbb_re_t, dbb_im_t):
    def body(lr_ref, li_ref, ldt_ref, br_ref, bi_ref, dar_ref, dai_ref, dbbr_ref, dbbi_ref,
             dlr_ref, dli_ref, dldt_ref, dbr_ref, dbi_ref):
        lr, li = lr_ref[...], li_ref[...]
        dt, mag, cs, sn, ab_re, ab_im, den, xr, co_re, co_im = _ssm_disc(lr, li, ldt_ref[...])
        br, bi, dbbr, dbbi = br_ref[...], bi_ref[...], dbbr_ref[...], dbbi_ref[...]
        dbr_ref[...] = co_re * dbbr + co_im * dbbi
        dbi_ref[...] = co_re * dbbi - co_im * dbbr
        dco_re = _colsum(br * dbbr + bi * dbbi)
        dco_im = _colsum(br * dbbi - bi * dbbr)
        dxr = (dco_re * lr - dco_im * li) / den
        dab_i = dai_ref[...] + (dco_re * li + dco_im * lr) / den
        dab_r = dar_ref[...] + dxr
        dden = -(co_re * dco_re + co_im * dco_im) / den
        dlr = (dco_re * xr + dco_im * ab_im) / den + dden * 2.0 * lr
        dli = (dco_re * ab_im - dco_im * xr) / den + dden * 2.0 * li
        dmag = dab_r * cs + dab_i * sn
        dth = mag * (dab_i * cs - dab_r * sn)
        dlr_ref[...] = dlr + dmag * mag * dt
        dli_ref[...] = dli + dth * dt
        dldt = jnp.broadcast_to((dmag * mag * lr + dth * li) * dt, (8, N_STATE))
        group = (lax.broadcasted_iota(jnp.int32, (N_STATE, 128), 0) // SSM_STATE
                 == lax.broadcasted_iota(jnp.int32, (N_STATE, 128), 1)).astype(F32)
        dldt_ref[...] = jnp.dot(dldt, group, precision=lax.Precision.HIGHEST, preferred_element_type=F32)

    vec, mat = _sds((1, N_STATE), F32), _sds((SSM_GROUP, N_STATE), F32)
    return pl.pallas_call(body, name="ssm_param_bwd", out_shape=(vec, vec, _sds((8, 128), F32), mat, mat))(
        lr, li, ldt, b_re_t, b_im_t, dab_re, dab_im, dbb_re_t, dbb_im_t)


ANY = pl.BlockSpec(memory_space=pl.ANY)


def _chip_peers(x, y):
    return [(1 - x, y), (x, 1 - y), (1 - x, 1 - y)]


def _gather_chips(buf):
    half = buf.shape[0] // 2

    def body(src_ref, out_ref, send_sems, recv_sems, pass_send_sems, pass_recv_sems, local_sem):
        x, y, c = lax.axis_index("x"), lax.axis_index("y"), lax.axis_index("c")
        me, sibling = 2 * x + y, (x, y, 1 - c)
        mine, theirs = pl.ds(c * half, half), pl.ds((1 - c) * half, half)
        peers = _chip_peers(x, y)
        local = pltpu.make_async_copy(src_ref, out_ref.at[me], local_sem)
        local.start()
        copies = []
        for k, (px, py) in enumerate(peers):
            copies.append(pltpu.make_async_remote_copy(src_ref.at[mine], out_ref.at[me, mine], send_sems.at[k],
                                                       recv_sems.at[k], device_id=(px, py, c), device_id_type=MESH))
            copies[-1].start()
        for k, (px, py) in enumerate(peers):
            landed = out_ref.at[2 * px + py, mine]
            pltpu.make_async_remote_copy(src_ref.at[mine], landed, send_sems.at[k], recv_sems.at[k],
                                         device_id=(px, py, c), device_id_type=MESH).wait_recv()
            copies.append(pltpu.make_async_remote_copy(landed, landed, pass_send_sems.at[k], pass_recv_sems.at[k],
                                                       device_id=sibling, device_id_type=MESH))
            copies[-1].start()
        for k, (px, py) in enumerate(peers):
            other = out_ref.at[2 * px + py, theirs]
            pltpu.make_async_remote_copy(other, other, pass_send_sems.at[k], pass_recv_sems.at[k], device_id=sibling,
                                         device_id_type=MESH).wait_recv()
        for cp in copies:
            cp.wait_send()
        local.wait()

    return pl.pallas_call(
        body, name="gather_chips", in_specs=[ANY], out_specs=ANY, out_shape=_sds((N_CHIPS,) + buf.shape, buf.dtype),
        scratch_shapes=[pltpu.SemaphoreType.DMA((3,)) for _ in range(4)] + [pltpu.SemaphoreType.DMA],
    )(buf)


def _scatter_chips(buf):
    def body(src_ref, out_ref, send_sems, recv_sems, local_sem):
        x, y, c = lax.axis_index("x"), lax.axis_index("y"), lax.axis_index("c")
        me = 2 * x + y
        local = pltpu.make_async_copy(src_ref.at[me], out_ref.at[me], local_sem)
        local.start()
        sends = []
        for k, (px, py) in enumerate(_chip_peers(x, y)):
            sends.append(pltpu.make_async_remote_copy(src_ref.at[2 * px + py], out_ref.at[me], send_sems.at[k],
                                                      recv_sems.at[k], device_id=(px, py, c), device_id_type=MESH))
            sends[-1].start()
        for k, (px, py) in enumerate(_chip_peers(x, y)):
            pltpu.make_async_remote_copy(src_ref.at[me], out_ref.at[2 * px + py], send_sems.at[k], recv_sems.at[k],
                                         device_id=(px, py, c), device_id_type=MESH).wait_recv()
        for cp in sends:
            cp.wait_send()
        local.wait()

    return pl.pallas_call(
        body, name="scatter_chips", in_specs=[ANY], out_specs=ANY, out_shape=_sds(buf.shape, buf.dtype),
        scratch_shapes=[pltpu.SemaphoreType.DMA((3,)), pltpu.SemaphoreType.DMA((3,)), pltpu.SemaphoreType.DMA],
    )(buf)


SWAP_CHUNKS = 8


def _swap_cores(buf):
    rows = buf.shape[0] // SWAP_CHUNKS

    def body(src_ref, out_ref, send_sems, recv_sems):
        x, y, c = lax.axis_index("x"), lax.axis_index("y"), lax.axis_index("c")
        copies = []
        for k in range(SWAP_CHUNKS):
            chunk = pl.ds(k * rows, rows)
            copies.append(pltpu.make_async_remote_copy(src_ref.at[chunk], out_ref.at[chunk], send_sems.at[k],
                                                       recv_sems.at[k], device_id=(x, y, 1 - c), device_id_type=MESH))
            copies[-1].start()
        for cp in copies:
            cp.wait_recv()
        for cp in copies:
            cp.wait_send()

    return pl.pallas_call(
        body, name="swap_cores", in_specs=[ANY], out_specs=ANY, out_shape=_sds(buf.shape, buf.dtype),
        scratch_shapes=[pltpu.SemaphoreType.DMA((SWAP_CHUNKS,)), pltpu.SemaphoreType.DMA((SWAP_CHUNKS,))],
    )(buf)


def _gather_all(buf):
    flips = [(dx, dy, dc) for dx in (0, 1) for dy in (0, 1) for dc in (0, 1)][1:]

    def body(src_ref, out_ref, send_sems, recv_sems, local_sem):
        x, y, c = lax.axis_index("x"), lax.axis_index("y"), lax.axis_index("c")
        me = 4 * x + 2 * y + c
        peers = [(1 - x if dx else x, 1 - y if dy else y, 1 - c if dc else c) for dx, dy, dc in flips]
        local = pltpu.make_async_copy(src_ref, out_ref.at[me], local_sem)
        local.start()
        sends = []
        for k, peer in enumerate(peers):
            sends.append(pltpu.make_async_remote_copy(src_ref, out_ref.at[me], send_sems.at[k], recv_sems.at[k],
                                                      device_id=peer, device_id_type=MESH))
            sends[-1].start()
        for k, (px, py, pc) in enumerate(peers):
            pltpu.make_async_remote_copy(src_ref, out_ref.at[4 * px + 2 * py + pc], send_sems.at[k], recv_sems.at[k],
                                         device_id=(px, py, pc), device_id_type=MESH).wait_recv()
        for cp in sends:
            cp.wait_send()
        local.wait()

    return pl.pallas_call(
        body, name="gather_all", in_specs=[ANY], out_specs=ANY, out_shape=_sds((N_DEV,) + buf.shape, buf.dtype),
        scratch_shapes=[pltpu.SemaphoreType.DMA((7,)), pltpu.SemaphoreType.DMA((7,)), pltpu.SemaphoreType.DMA],
    )(buf)


PACK_W = 1024
PACK_ROWS = 256


def _sum_parts(parts):
    n, r, w = parts.shape

    def body(p_ref, o_ref):
        acc = p_ref[0].astype(F32)
        for k in range(1, n):
            acc = acc + p_ref[k].astype(F32)
        o_ref[...] = acc

    return pl.pallas_call(
        body, name="sum_parts", grid=(r // PACK_ROWS,),
        in_specs=[pl.BlockSpec((n, PACK_ROWS, w), lambda i: (0, i, 0))], out_specs=_rows(PACK_ROWS, w),
        out_shape=_sds((r, w), F32), compiler_params=_params(("parallel",)),
    )(parts)


def _adamw(parts, w, m, v):
    n = len(parts)
    r, wd = w.shape

    def body(*refs):
        w_ref, m_ref, v_ref, g_ref, dw_ref, nm_ref, nv_ref = refs[n:]
        g = refs[0][...]
        for k in range(1, n):
            g = g + refs[k][...]
        g_ref[...] = g
        m_new = ADAM_B1 * m_ref[...] + (1.0 - ADAM_B1) * g
        v_new = ADAM_B2 * v_ref[...] + (1.0 - ADAM_B2) * (g * g)
        nm_ref[...] = m_new
        nv_ref[...] = v_new
        m_hat = m_new / (1.0 - ADAM_B1 ** ADAM_STEP)
        v_hat = v_new / (1.0 - ADAM_B2 ** ADAM_STEP)
        dw_ref[...] = -ADAM_LR * (m_hat / (jnp.sqrt(v_hat) + ADAM_EPS) + ADAM_WD * w_ref[...])

    blk = _rows(PACK_ROWS, wd)
    return pl.pallas_call(
        body, name="adamw", grid=(r // PACK_ROWS,),
        in_specs=[blk] * (n + 3), out_specs=(blk,) * 4,
        out_shape=(_sds((r, wd), F32),) * 4, compiler_params=_params(("parallel",)),
    )(*parts, w, m, v)


PACK_UNIT = PACK_W * 16


def _pack(arrs, dtype):
    unit = PACK_UNIT
    flat = []
    for a in arrs:
        a = a.reshape(-1).astype(dtype)
        flat.append(jnp.pad(a, (0, -a.size % unit)))
    total = sum(f.size for f in flat)
    flat.append(jnp.zeros((-total % (PACK_W * PACK_ROWS),), dtype))
    return jnp.concatenate(flat).reshape(-1, PACK_W)


def _unpack(buf, shapes):
    unit = PACK_UNIT
    flat = buf.reshape(-1)
    out, off = [], 0
    for shp in shapes:
        size = 1
        for s in shp:
            size *= s
        out.append(flat[off:off + size].reshape(shp))
        off += size + (-size % unit)
    return out


def _expand_bd(m_t):
    g_row = jnp.arange(D_SSM)[:, None] // SSM_GROUP
    g_col = jnp.arange(N_STATE)[None, :] // SSM_STATE
    return jnp.where(g_row == g_col, jnp.tile(m_t, (SSM_GROUPS, 1)), 0.0)


def _extract_bd(full):
    g_row = jnp.arange(D_SSM)[:, None] // SSM_GROUP
    g_col = jnp.arange(N_STATE)[None, :] // SSM_STATE
    return jnp.where(g_row == g_col, full, 0.0).reshape(SSM_GROUPS, SSM_GROUP, N_STATE).sum(0)


def _heads_rows(a):
    return a.reshape(a.shape[0], SB_HEADS, SB_HEAD_DIM).transpose(1, 0, 2)


def _heads_blocks_t(a):
    return a.reshape(a.shape[0] // BQ, BQ, SB_HEADS, SB_HEAD_DIM).transpose(2, 0, 3, 1)


def _heads_blocks_r(a):
    return a.reshape(a.shape[0] // BQ, BQ, SB_HEADS, SB_HEAD_DIM).transpose(2, 0, 1, 3)


def _prep_layer(w, l):
    p = {}
    for name in ('norm_mix_pre', 'norm_mix_post', 'b_gate', 'norm_ffn_pre', 'norm_ffn_post', 'ssm_d'):
        p[name] = w[name][l][None, :]
    for name in ('w_in', 'w_gate', 'w_glu_val', 'w_glu_gate', 'w_attn_out', 'w_conv_out', 'w_mix_out', 'w_ffn_up',
                 'w_ffn_down'):
        p[name] = w[name][l]
        p[name + '_t'] = w[name][l].T
    p['conv_w'] = w['conv_w'][l]
    p['lr'] = w['ssm_a_re'][l].reshape(1, N_STATE)
    p['li'] = w['ssm_a_im'][l].reshape(1, N_STATE)
    p['ldt'] = jnp.repeat(w['ssm_log_dt'][l], SSM_STATE).reshape(1, N_STATE)
    p['b_re_t'] = w['ssm_b_re'][l].transpose(2, 0, 1).reshape(SSM_GROUP, N_STATE)
    p['b_im_t'] = w['ssm_b_im'][l].transpose(2, 0, 1).reshape(SSM_GROUP, N_STATE)
    p['ab_re'], p['ab_im'], bb_re_t, bb_im_t = _ssm_param_fwd(p['lr'], p['li'], p['ldt'], p['b_re_t'], p['b_im_t'])
    p['bb_re'] = _expand_bd(bb_re_t).astype(BF16)
    p['bb_im'] = _expand_bd(bb_im_t).astype(BF16)
    p['ct_re'] = _expand_bd(w['ssm_c_re'][l].transpose(1, 0, 2).reshape(SSM_GROUP, N_STATE)).astype(BF16)
    p['ct_im'] = _expand_bd(w['ssm_c_im'][l].transpose(1, 0, 2).reshape(SSM_GROUP, N_STATE)).astype(BF16)
    return p


def _layer_fwd(x, p):
    s = {'x': x}
    s['hb'], s['u'], q, k, v, s['cb'], s['cc'], s['cx'], s['gate'] = _fwd_in(
        x, p['norm_mix_pre'], p['w_in'], p['w_gate'], p['b_gate'])
    s['h_re'], s['h_im'], s['y_pre'] = _ssm_fwd(s['u'], p['bb_re'], p['bb_im'], p['ab_re'], p['ab_im'], p['ct_re'].T,
                                                p['ct_im'].T, p['ssm_d'])
    s['q'], s['k'], s['v'] = q, k, v
    o_r, s['r_tab'] = _attn_fwd(_heads_rows(q), _heads_blocks_t(k), _heads_blocks_r(v))
    s['o'] = o_r.transpose(1, 0, 2).reshape(x.shape[0], D_SB).astype(BF16)
    s['x1'], s['merged'], s['m2'] = _merge_fwd(
        s['y_pre'], s['o'], s['cb'], s['cc'], s['cx'], p['conv_w'], s['gate'], x, p['w_glu_val'], p['w_glu_gate'],
        p['w_attn_out'], p['w_conv_out'], p['w_mix_out'], p['norm_mix_post'])
    x2, s['h2'], s['f'] = _ffn_fwd(s['x1'], p['norm_ffn_pre'], p['norm_ffn_post'], p['w_ffn_up'], p['w_ffn_down'])
    return x2, s


def _layer_bwd(dx2, p, s):
    t = dx2.shape[0]
    g = {}
    dx1, da, r, df, g['norm_ffn_pre'], g['norm_ffn_post'] = _ffn_bwd(
        dx2, s['f'], s['h2'], s['x1'], p['norm_ffn_pre'], p['norm_ffn_post'], p['w_ffn_up'], p['w_ffn_up_t'],
        p['w_ffn_down_t'])
    g['w_ffn_up'] = _mm_tn(s['h2'], da)
    g['w_ffn_down'] = _mm_tn(r, df)

    (dgp, dy_pre, do, dcb, dyconv, dm2, d_val, d_glu, yg, dy_b, dy_c, cy, g['norm_mix_post'], g['b_gate']) = _merge_bwd(
        dx1, s['m2'], p['norm_mix_post'], p['w_mix_out_t'], s['gate'], s['y_pre'], s['o'], s['cb'], s['cc'], s['cx'],
        p['conv_w'], p['w_glu_val'], p['w_glu_gate'], p['w_attn_out'], p['w_conv_out'], p['w_glu_val_t'],
        p['w_glu_gate_t'], p['w_attn_out_t'], p['w_conv_out_t'])
    g['w_mix_out'] = _mm_tn(s['merged'], dm2)
    g['w_glu_val'] = _mm_tn(yg, d_val)
    g['w_glu_gate'] = _mm_tn(yg, d_glu)
    g['w_attn_out'] = _mm_tn(s['o'], dy_b)
    g['w_conv_out'] = _mm_tn(cy, dy_c)

    dcc, dcx, dcw = _conv_bwd(dyconv, s['cc'], s['cx'], p['conv_w'])
    g['conv_w'] = dcw[0:3]

    dq_r, dk_t, dv_t = _attn_bwd(_heads_rows(s['q']), _heads_blocks_t(s['q']), _heads_rows(do), _heads_blocks_t(do),
                                 _heads_blocks_t(s['k']), _heads_blocks_r(s['k']), _heads_blocks_t(s['v']), s['r_tab'])
    dq = dq_r.transpose(1, 0, 2).reshape(t, D_SB)
    dk = dk_t.transpose(1, 3, 0, 2).reshape(t, D_SB)
    dv = dv_t.transpose(1, 3, 0, 2).reshape(t, D_SB)

    du, dbb_re, dbb_im, dct_re, dct_im, dab_re, dab_im, g['ssm_d'] = _ssm_bwd(
        dy_pre, s['u'], s['h_re'], s['h_im'], p['ct_re'], p['ct_im'], p['bb_re'].T, p['bb_im'].T, p['ab_re'],
        p['ab_im'], p['ssm_d'])
    dlr, dli, dldt, db_re_t, db_im_t = _ssm_param_bwd(p['lr'], p['li'], p['ldt'], p['b_re_t'], p['b_im_t'], dab_re,
                                                      dab_im, _extract_bd(dbb_re), _extract_bd(dbb_im))
    g['ssm_a_re'] = dlr.reshape(SSM_GROUPS, SSM_STATE)
    g['ssm_a_im'] = dli.reshape(SSM_GROUPS, SSM_STATE)
    g['ssm_log_dt'] = dldt[0, :SSM_GROUPS]
    g['ssm_b_re'] = db_re_t.reshape(SSM_GROUP, SSM_GROUPS, SSM_STATE).transpose(1, 2, 0)
    g['ssm_b_im'] = db_im_t.reshape(SSM_GROUP, SSM_GROUPS, SSM_STATE).transpose(1, 2, 0)
    g['ssm_c_re'] = _extract_bd(dct_re).reshape(SSM_GROUP, SSM_GROUPS, SSM_STATE).transpose(1, 0, 2)
    g['ssm_c_im'] = _extract_bd(dct_im).reshape(SSM_GROUP, SSM_GROUPS, SSM_STATE).transpose(1, 0, 2)

    dp = jnp.concatenate([du.astype(BF16), dq.astype(BF16), dk.astype(BF16), dv.astype(BF16), dcb.astype(BF16),
                          dcc.astype(BF16), dcx.astype(BF16)], axis=1)
    dx, g['norm_mix_pre'] = _in_bwd(dp, dgp, p['w_in_t'], p['w_gate_t'], s['x'], p['norm_mix_pre'], dx1)
    g['w_in'] = _mm_tn(s['hb'], dp)
    g['w_gate'] = _mm_tn(s['hb'], dgp)
    for name in ('norm_mix_pre', 'norm_mix_post', 'b_gate', 'norm_ffn_pre', 'norm_ffn_post', 'ssm_d'):
        g[name] = g[name][0]
    return dx, g


def _local_step(x, target, w):
    ps, saved = [], []
    for l in range(DEPTH):
        ps.append(_prep_layer(w, l))
        x, s = _layer_fwd(x, ps[l])
        saved.append(s)
    dx, loss_blk = _loss_grad(x, target)
    grads = [None] * DEPTH
    for l in reversed(range(DEPTH)):
        dx, grads[l] = _layer_bwd(dx, ps[l], saved[l])
    return loss_blk, dx, {n: jnp.stack([grads[l][n] for l in range(DEPTH)]) for n in WEIGHTS}


def kernel(x, norm_mix_pre, norm_mix_post, w_in, w_gate, b_gate, ssm_a_re, ssm_a_im, ssm_log_dt, ssm_b_re,
           ssm_b_im, ssm_c_re, ssm_c_im, ssm_d, w_glu_val, w_glu_gate, w_attn_out, conv_w, w_conv_out,
           w_mix_out, norm_ffn_pre, norm_ffn_post, w_ffn_up, w_ffn_down, loss_target, m_norm_mix_pre,
           m_norm_mix_post, m_w_in, m_w_gate, m_b_gate, m_ssm_a_re, m_ssm_a_im, m_ssm_log_dt, m_ssm_b_re,
           m_ssm_b_im, m_ssm_c_re, m_ssm_c_im, m_ssm_d, m_w_glu_val, m_w_glu_gate, m_w_attn_out, m_conv_w,
           m_w_conv_out, m_w_mix_out, m_norm_ffn_pre, m_norm_ffn_post, m_w_ffn_up, m_w_ffn_down,
           v_norm_mix_pre, v_norm_mix_post, v_w_in, v_w_gate, v_b_gate, v_ssm_a_re, v_ssm_a_im, v_ssm_log_dt,
           v_ssm_b_re, v_ssm_b_im, v_ssm_c_re, v_ssm_c_im, v_ssm_d, v_w_glu_val, v_w_glu_gate, v_w_attn_out,
           v_conv_w, v_w_conv_out, v_w_mix_out, v_norm_ffn_pre, v_norm_ffn_post, v_w_ffn_up, v_w_ffn_down):
    given = dict(locals())
    shard = {n: given[n] for n in WEIGHTS}
    big = list(SHARDED)

    sent = [lax.bitcast_convert_type(shard[n], BF16) if n == 'conv_w' else shard[n].astype(BF16) for n in big]
    gathered = _gather_chips(_pack(sent, BF16))
    full = {n: shard[n] for n in REPLICATED}
    per_chip = [_unpack(gathered[j], [a.shape for a in sent]) for j in range(N_CHIPS)]
    for i, n in enumerate(big):
        parts = [per_chip[j][i] for j in range(N_CHIPS)]
        if n == 'conv_w':
            parts = [lax.bitcast_convert_type(a, F32) for a in parts]
        full[n] = jnp.concatenate(parts, axis=SHARDED[n])

    loss_blk, dx, grads = _local_step(x[0], loss_target[0], full)
    loss = lax.psum(loss_blk[0, 0], ("x", "y", "c"))

    to_chip = [_pack([jnp.split(grads[n], N_CHIPS, axis=SHARDED[n])[j] for n in big], BF16) for j in range(N_CHIPS)]
    core_sum = _sum_parts(_scatter_chips(jnp.stack(to_chip)))
    out_big = _adamw([core_sum, _swap_cores(core_sum)], _pack([shard[n] for n in big], F32),
                     _pack([given['m_' + n] for n in big], F32), _pack([given['v_' + n] for n in big], F32))
    all_parts = _gather_all(_pack([grads[n] for n in REPLICATED], F32))
    out_small = _adamw([all_parts[k] for k in range(N_DEV)], _pack([shard[n] for n in REPLICATED], F32),
                       _pack([given['m_' + n] for n in REPLICATED], F32),
                       _pack([given['v_' + n] for n in REPLICATED], F32))

    results = []
    for kind in range(4):
        got = dict(zip(big, _unpack(out_big[kind], [shard[n].shape for n in big])))
        got.update(zip(REPLICATED, _unpack(out_small[kind], [shard[n].shape for n in REPLICATED])))
        results += [got[n] for n in WEIGHTS]
    return (loss, dx[None], *results)
```

```python
import functools

import jax
import jax.numpy as jnp
from jax import lax
from jax.experimental import pallas as pl
from jax.experimental.pallas import tpu as pltpu

F32, BF16 = jnp.float32, jnp.bfloat16

D_MODEL = 1024
DEPTH = 2
SSM_GROUPS, SSM_GROUP, SSM_STATE = 16, 16, 64
D_SSM = SSM_GROUPS * SSM_GROUP
N_STATE = SSM_GROUPS * SSM_STATE
SB_HEADS, SB_HEAD_DIM = 8, 64
D_SB = SB_HEADS * SB_HEAD_DIM
D_CONV = 256
D_IN = D_SSM + 3 * D_SB + 3 * D_CONV
D_GATE = 3 * D_MODEL
D_FF = 4 * D_MODEL
EPS = 1e-6
SB_SCALE = SB_HEAD_DIM ** -0.5
GELU_C = 0.7978845608028654
GELU_A = 0.044715

ADAM_LR, ADAM_B1, ADAM_B2, ADAM_EPS, ADAM_WD, ADAM_STEP = 0.001, 0.9, 0.999, 1e-08, 0.01, 10

TM = 256
BQ = 256
SB_UNROLL = 4
R_LANES = 128
HALO = 8
VMEM_LIMIT_MB = 56

MESH = pl.DeviceIdType.MESH
WEIGHTS = ['norm_mix_pre', 'norm_mix_post', 'w_in', 'w_gate', 'b_gate', 'ssm_a_re', 'ssm_a_im', 'ssm_log_dt',
           'ssm_b_re', 'ssm_b_im', 'ssm_c_re', 'ssm_c_im', 'ssm_d', 'w_glu_val', 'w_glu_gate', 'w_attn_out', 'conv_w',
           'w_conv_out', 'w_mix_out', 'norm_ffn_pre', 'norm_ffn_post', 'w_ffn_up', 'w_ffn_down']
SHARDED = {'w_in': 2, 'w_gate': 2, 'w_glu_val': 2, 'w_glu_gate': 2, 'w_attn_out': 2, 'conv_w': 2, 'w_conv_out': 2,
           'w_mix_out': 1, 'w_ffn_up': 2, 'w_ffn_down': 1}
REPLICATED = [n for n in WEIGHTS if n not in SHARDED]
N_CHIPS = 4
N_DEV = 8


def _dot(a, b):
    return jnp.dot(a, b, preferred_element_type=F32)


def _dot_tn(a, b):
    return lax.dot_general(a, b, (((0,), (0,)), ((), ())), preferred_element_type=F32)


def _sigmoid(x):
    return 1.0 / (1.0 + jnp.exp(-x))


def _params(sem, vmem_mb=VMEM_LIMIT_MB):
    return pltpu.CompilerParams(dimension_semantics=sem, vmem_limit_bytes=vmem_mb << 20)


def _rows(tm, n):
    return pl.BlockSpec((tm, n), lambda i: (i, 0))


def _whole(shape):
    zeros = (0,) * len(shape)
    return pl.BlockSpec(shape, lambda *_: zeros)


def _sds(shape, dtype):
    return jax.ShapeDtypeStruct(shape, dtype)


def _rms_fwd(x, g):
    r = lax.rsqrt(jnp.mean(x * x, axis=-1, keepdims=True) + EPS)
    return x * r * g


def _rms_bwd(x, g, dy):
    r = lax.rsqrt(jnp.mean(x * x, axis=-1, keepdims=True) + EPS)
    xh = x * r
    dxh = dy * g
    dx = r * (dxh - xh * jnp.mean(dxh * xh, axis=-1, keepdims=True))
    return dx, dy * xh


def _colsum(a):
    return jnp.sum(a, axis=0, keepdims=True)


def _gelu(y):
    return 0.5 * y * (1.0 + jnp.tanh(GELU_C * (y + GELU_A * y * y * y)))


def _gelu_grad(y):
    th = jnp.tanh(GELU_C * (y + GELU_A * y * y * y))
    return 0.5 * (1.0 + th) + 0.5 * y * (1.0 - th * th) * GELU_C * (1.0 + 3.0 * GELU_A * y * y)


def _fwd_in(x, g_pre, w_in, w_gate, b_gate):
    t = x.shape[0]

    def body(x_ref, g_ref, win_ref, wg_ref, bg_ref, hb_ref, u_ref, q_ref, k_ref, v_ref, cb_ref, cc_ref, cx_ref, gate_ref):
        hb = _rms_fwd(x_ref[...], g_ref[...]).astype(BF16)
        hb_ref[...] = hb
        p = _dot(hb, win_ref[...])
        o = 0
        u_ref[...] = p[:, o:o + D_SSM]
        o += D_SSM
        q_ref[...] = (p[:, o:o + D_SB] * -SB_SCALE).astype(BF16)
        o += D_SB
        k_ref[...] = p[:, o:o + D_SB].astype(BF16)
        o += D_SB
        v_ref[...] = p[:, o:o + D_SB].astype(BF16)
        o += D_SB
        cb_ref[...] = p[:, o:o + D_CONV]
        o += D_CONV
        cc_ref[...] = p[:, o:o + D_CONV]
        o += D_CONV
        cx_ref[...] = p[:, o:o + D_CONV]
        gate_ref[...] = _sigmoid(_dot(hb, wg_ref[...]) + bg_ref[...])

    outs = (_sds((t, D_MODEL), BF16), _sds((t, D_SSM), F32), _sds((t, D_SB), BF16), _sds((t, D_SB), BF16),
            _sds((t, D_SB), BF16), _sds((t, D_CONV), F32), _sds((t, D_CONV), F32), _sds((t, D_CONV), F32),
            _sds((t, D_GATE), F32))
    return pl.pallas_call(
        body, name="fwd_in", grid=(t // TM,),
        in_specs=[_rows(TM, D_MODEL), _whole((1, D_MODEL)), _whole((D_MODEL, D_IN)), _whole((D_MODEL, D_GATE)),
                  _whole((1, D_GATE))],
        out_specs=tuple(_rows(TM, s.shape[1]) for s in outs), out_shape=outs,
        compiler_params=_params(("parallel",)),
    )(x, g_pre, w_in, w_gate, b_gate)


def _ssm_scan_rows(n_rows, ar, ai, sign, in_re_ref, in_im_ref, out_re_ref, out_im_ref, carry_re_ref, carry_im_ref,
                   reverse):
    def group(gi, carry):
        hr, hi = carry
        g = (n_rows // 8 - 1 - gi) if reverse else gi
        r0 = pl.multiple_of(g * 8, 8)
        cr = in_re_ref[pl.ds(r0, 8), :]
        ci = in_im_ref[pl.ds(r0, 8), :]
        outs_r, outs_i = [None] * 8, [None] * 8
        for kk in range(8):
            k = 7 - kk if reverse else kk
            nr = ar * hr - sign * ai * hi + cr[k:k + 1, :]
            ni = ar * hi + sign * ai * hr + ci[k:k + 1, :]
            hr, hi = nr, ni
            outs_r[k], outs_i[k] = hr, hi
        out_re_ref[pl.ds(r0, 8), :] = jnp.concatenate(outs_r, axis=0)
        out_im_ref[pl.ds(r0, 8), :] = jnp.concatenate(outs_i, axis=0)
        return hr, hi

    hr, hi = lax.fori_loop(0, n_rows // 8, group, (carry_re_ref[...], carry_im_ref[...]))
    carry_re_ref[...] = hr
    carry_im_ref[...] = hi


def _ssm_fwd(u, bb_re, bb_im, ab_re, ab_im, c_re, c_im, d_skip):
    t = u.shape[0]

    def body(u_ref, bre_ref, bim_ref, ar_ref, ai_ref, cre_ref, cim_ref, d_ref, hre_ref, him_ref, y_ref,
             bur_ref, bui_ref, car_ref, cai_ref):
        @pl.when(pl.program_id(0) == 0)
        def _():
            car_ref[...] = jnp.zeros_like(car_ref)
            cai_ref[...] = jnp.zeros_like(cai_ref)

        uv = u_ref[...]
        ub = uv.astype(BF16)
        bur_ref[...] = _dot(ub, bre_ref[...])
        bui_ref[...] = _dot(ub, bim_ref[...])
        _ssm_scan_rows(TM, ar_ref[...], ai_ref[...], 1.0, bur_ref, bui_ref, hre_ref, him_ref, car_ref, cai_ref, False)
        y_ref[...] = (_dot(hre_ref[...].astype(BF16), cre_ref[...]) - _dot(him_ref[...].astype(BF16), cim_ref[...])
                      + d_ref[...] * uv)

    outs = (_sds((t, N_STATE), F32), _sds((t, N_STATE), F32), _sds((t, D_SSM), F32))
    return pl.pallas_call(
        body, name="ssm_fwd", grid=(t // TM,),
        in_specs=[_rows(TM, D_SSM), _whole((D_SSM, N_STATE)), _whole((D_SSM, N_STATE)), _whole((1, N_STATE)),
                  _whole((1, N_STATE)), _whole((N_STATE, D_SSM)), _whole((N_STATE, D_SSM)), _whole((1, D_SSM))],
        out_specs=(_rows(TM, N_STATE), _rows(TM, N_STATE), _rows(TM, D_SSM)), out_shape=outs,
        scratch_shapes=[pltpu.VMEM((TM, N_STATE), F32), pltpu.VMEM((TM, N_STATE), F32),
                        pltpu.VMEM((1, N_STATE), F32), pltpu.VMEM((1, N_STATE), F32)],
        compiler_params=_params(("arbitrary",)),
    )(u, bb_re, bb_im, ab_re, ab_im, c_re, c_im, d_skip)


def _neg_abs(x):
    return lax.bitcast_convert_type(lax.bitcast_convert_type(x, jnp.int32) | jnp.int32(-2 ** 31), F32)


def _sb_log1m(nz, mask):
    b = jnp.minimum(nz, 0.0) - jnp.log(1.0 + jnp.exp(_neg_abs(nz)))
    return b if mask is None else jnp.where(mask, b, 0.0)


def _sb_weights(nz, b, incl, r_run, mask):
    w = jnp.exp((r_run + _dot(b.astype(BF16), incl)) - nz)
    return w if mask is None else jnp.where(mask, w, 0.0)


def _rowsum(a):
    return jnp.sum(a, axis=1, keepdims=True)


def _attn_fwd(q_r, k_t, v_r):
    h, t, _ = q_r.shape
    nk = t // BQ
    assert nk <= R_LANES

    def body(q_ref, kt_ref, v_ref, o_ref, rtab_ref):
        i = pl.program_id(1)
        q = q_ref[0]
        rows = lax.broadcasted_iota(jnp.int32, (BQ, BQ), 0)
        cols = lax.broadcasted_iota(jnp.int32, (BQ, BQ), 1)
        tri = rows > cols
        incl = (rows >= cols).astype(BF16)
        lane = lax.broadcasted_iota(jnp.int32, (BQ, R_LANES), 1)

        def group(js, carry, mask=None):
            r_run, acc, rtab = carry
            nzs = [_dot(q, kt_ref[0, j]) for j in js]
            bs = [_sb_log1m(nz, mask) for nz in nzs]
            rs = [r_run]
            for b in bs:
                rs.append(rs[-1] + _rowsum(b))
            ws = [_sb_weights(nz, b, incl, r, mask) for nz, b, r in zip(nzs, bs, rs)]
            for j, w, r in zip(js, ws, rs):
                acc = acc + _dot(w.astype(BF16), v_ref[0, j])
                rtab = jnp.where(lane == j, r, rtab)
            return rs[-1], acc, rtab

        carry = (jnp.zeros((BQ, 1), F32), jnp.zeros((BQ, SB_HEAD_DIM), F32), jnp.zeros((BQ, R_LANES), F32))
        carry = group([i], carry, tri)
        rem = i % SB_UNROLL
        carry = lax.fori_loop(0, rem % 2, lambda n, c: group([i - 1], c), carry)
        carry = lax.fori_loop(0, rem // 2, lambda n, c: group([i - 1 - rem % 2 - 2 * n - k for k in range(2)], c), carry)
        carry = lax.fori_loop(0, i // SB_UNROLL,
                              lambda m, c: group([i - 1 - rem - SB_UNROLL * m - k for k in range(SB_UNROLL)], c), carry)
        o_ref[0] = carry[1]
        rtab_ref[0] = carry[2]

    return pl.pallas_call(
        body, name="attn_fwd", grid=(h, nk),
        in_specs=[pl.BlockSpec((1, BQ, SB_HEAD_DIM), lambda hh, i: (hh, i, 0)),
                  pl.BlockSpec((1, nk, SB_HEAD_DIM, BQ), lambda hh, i: (hh, 0, 0, 0)),
                  pl.BlockSpec((1, nk, BQ, SB_HEAD_DIM), lambda hh, i: (hh, 0, 0, 0))],
        out_specs=(pl.BlockSpec((1, BQ, SB_HEAD_DIM), lambda hh, i: (hh, i, 0)),
                   pl.BlockSpec((1, BQ, R_LANES), lambda hh, i: (hh, i, 0))),
        out_shape=(_sds((h, t, SB_HEAD_DIM), F32), _sds((h, t, R_LANES), F32)),
        compiler_params=_params(("parallel", "parallel")),
    )(q_r, k_t, v_r)


def _merge_fwd(y_pre, o, cb, cc, cx, conv_w, gate, x, w_val, w_glu, w_ao, w_co, w_mo, g_post):
    t = x.shape[0]

    def body(y_ref, o_ref, cb_ref, cc_ref, cx_ref, ccp_ref, cxp_ref, cw_ref, gate_ref, x_ref, wv_ref, wg_ref, wao_ref,
             wco_ref, wmo_ref, gp_ref, x1_ref, mg_ref, m2_ref):
        i = pl.program_id(0)
        ygb = _gelu(y_ref[...]).astype(BF16)
        y_a = _dot(ygb, wv_ref[...]) * _sigmoid(_dot(ygb, wg_ref[...]))
        y_b = _dot(o_ref[...], wao_ref[...])
        yconv = _conv_fwd(cc_ref[...] * cx_ref[...], jnp.where(i > 0, ccp_ref[...] * cxp_ref[...], 0.0), cw_ref[...])[0]
        y_c = _dot((cb_ref[...] * yconv).astype(BF16), wco_ref[...])
        gate_v = gate_ref[...]
        merged = (gate_v[:, :D_MODEL] * y_a + gate_v[:, D_MODEL:2 * D_MODEL] * y_b
                  + gate_v[:, 2 * D_MODEL:] * y_c).astype(BF16)
        mg_ref[...] = merged
        m2 = _dot(merged, wmo_ref[...])
        m2_ref[...] = m2
        x1_ref[...] = x_ref[...] + _rms_fwd(m2, gp_ref[...])

    outs = (_sds((t, D_MODEL), F32), _sds((t, D_MODEL), BF16), _sds((t, D_MODEL), F32))
    return pl.pallas_call(
        body, name="merge_fwd", grid=(t // TM,),
        in_specs=[_rows(TM, D_SSM), _rows(TM, D_SB), _rows(TM, D_CONV), _rows(TM, D_CONV), _rows(TM, D_CONV),
                  _halo_before(D_CONV), _halo_before(D_CONV), _whole((3, D_CONV)), _rows(TM, D_GATE),
                  _rows(TM, D_MODEL), _whole((D_SSM, D_MODEL)), _whole((D_SSM, D_MODEL)), _whole((D_SB, D_MODEL)),
                  _whole((D_CONV, D_MODEL)), _whole((D_MODEL, D_MODEL)), _whole((1, D_MODEL))],
        out_specs=tuple(_rows(TM, D_MODEL) for _ in outs), out_shape=outs,
        compiler_params=_params(("parallel",)),
    )(y_pre, o, cb, cc, cx, cc, cx, conv_w, gate, x, w_val, w_glu, w_ao, w_co, w_mo, g_post)


def _halo_before(n):
    return pl.BlockSpec((HALO, n), lambda i: (jnp.maximum(i * (TM // HALO) - 1, 0), 0))


def _halo_after(n, n_tiles):
    return pl.BlockSpec((HALO, n), lambda i: (jnp.minimum((i + 1) * (TM // HALO), n_tiles * (TM // HALO) - 1), 0))


def _conv_fwd(z, z_before, cw):
    row = lax.broadcasted_iota(jnp.int32, z.shape, 0)
    z1 = jnp.where(row == 0, z_before[HALO - 1:HALO, :], pltpu.roll(z, 1, axis=0))
    z2 = jnp.where(row == 0, z_before[HALO - 2:HALO - 1, :],
                   jnp.where(row == 1, z_before[HALO - 1:HALO, :], pltpu.roll(z, 2, axis=0)))
    return cw[0:1, :] * z2 + cw[1:2, :] * z1 + cw[2:3, :] * z, z1, z2


def _ffn_fwd(x1, g1, g2, w_up, w_down):
    t = x1.shape[0]

    def body(x_ref, g1_ref, g2_ref, wu_ref, wd_ref, x2_ref, h2_ref, f_ref):
        xv = x_ref[...]
        h2 = _rms_fwd(xv, g1_ref[...]).astype(BF16)
        h2_ref[...] = h2
        ra = jnp.maximum(_dot(h2, wu_ref[...]), 0.0)
        f = _dot((ra * ra).astype(BF16), wd_ref[...])
        f_ref[...] = f
        x2_ref[...] = xv + _rms_fwd(f, g2_ref[...])

    outs = (_sds((t, D_MODEL), F32), _sds((t, D_MODEL), BF16), _sds((t, D_MODEL), F32))
    return pl.pallas_call(
        body, name="ffn_fwd", grid=(t // TM,),
        in_specs=[_rows(TM, D_MODEL), _whole((1, D_MODEL)), _whole((1, D_MODEL)), _whole((D_MODEL, D_FF)),
                  _whole((D_FF, D_MODEL))],
        out_specs=tuple(_rows(TM, D_MODEL) for _ in outs), out_shape=outs,
        compiler_params=_params(("parallel",)),
    )(x1, g1, g2, w_up, w_down)


def _loss_grad(y, target):
    t = y.shape[0]

    def body(y_ref, t_ref, dy_ref, loss_ref):
        @pl.when(pl.program_id(0) == 0)
        def _():
            loss_ref[...] = jnp.zeros_like(loss_ref)

        err = y_ref[...] - t_ref[...]
        dy_ref[...] = err * (1.0 / D_MODEL)
        loss_ref[...] += 0.5 * jnp.sum(jnp.mean(err * err, axis=-1, keepdims=True), axis=0, keepdims=True)

    return pl.pallas_call(
        body, name="loss_grad", grid=(t // TM,),
        in_specs=[_rows(TM, D_MODEL), _rows(TM, D_MODEL)],
        out_specs=(_rows(TM, D_MODEL), _whole((8, 128))),
        out_shape=(_sds((t, D_MODEL), F32), _sds((8, 128), F32)),
        compiler_params=_params(("arbitrary",)),
    )(y, target)


def _ffn_bwd(dx2, f, h2, x1, g1, g2, w_up, w_up_t, w_down_t):
    t = x1.shape[0]

    def body(dx2_ref, f_ref, h2_ref, x1_ref, g1_ref, g2_ref, wu_ref, wut_ref, wdt_ref, dx1_ref, da_ref, r_ref, df_ref,
             dg1_ref, dg2_ref):
        @pl.when(pl.program_id(0) == 0)
        def _():
            dg1_ref[...] = jnp.zeros_like(dg1_ref)
            dg2_ref[...] = jnp.zeros_like(dg2_ref)

        dx2 = dx2_ref[...]
        df, dg2 = _rms_bwd(f_ref[...], g2_ref[...], dx2)
        dg2_ref[...] += _colsum(dg2)
        dfb = df.astype(BF16)
        df_ref[...] = dfb
        ra = jnp.maximum(_dot(h2_ref[...], wu_ref[...]), 0.0)
        r_ref[...] = (ra * ra).astype(BF16)
        da = (_dot(dfb, wdt_ref[...]) * (2.0 * ra)).astype(BF16)
        da_ref[...] = da
        dxn, dg1 = _rms_bwd(x1_ref[...], g1_ref[...], _dot(da, wut_ref[...]))
        dg1_ref[...] += _colsum(dg1)
        dx1_ref[...] = dx2 + dxn

    outs = (_sds((t, D_MODEL), F32), _sds((t, D_FF), BF16), _sds((t, D_FF), BF16), _sds((t, D_MODEL), BF16),
            _sds((1, D_MODEL), F32), _sds((1, D_MODEL), F32))
    return pl.pallas_call(
        body, name="ffn_bwd", grid=(t // TM,),
        in_specs=[_rows(TM, D_MODEL), _rows(TM, D_MODEL), _rows(TM, D_MODEL), _rows(TM, D_MODEL), _whole((1, D_MODEL)),
                  _whole((1, D_MODEL)), _whole((D_MODEL, D_FF)), _whole((D_FF, D_MODEL)), _whole((D_MODEL, D_FF))],
        out_specs=(_rows(TM, D_MODEL), _rows(TM, D_FF), _rows(TM, D_FF), _rows(TM, D_MODEL), _whole((1, D_MODEL)),
                   _whole((1, D_MODEL))),
        out_shape=outs, compiler_params=_params(("arbitrary",), 60),
    )(dx2, f, h2, x1, g1, g2, w_up, w_up_t, w_down_t)


def _mm_tn(a, b):
    t, k = a.shape
    n = b.shape[1]
    tk, tt = min(k, 1024), min(t, 1024)
    tn = next(c for c in (1024, 512, 256) if n % c == 0)

    def body(a_ref, b_ref, o_ref):
        @pl.when(pl.program_id(2) == 0)
        def _():
            o_ref[...] = jnp.zeros_like(o_ref)

        o_ref[...] += _dot_tn(a_ref[...], b_ref[...])

    return pl.pallas_call(
        body, name="mm_tn", grid=(k // tk, n // tn, t // tt),
        in_specs=[pl.BlockSpec((tt, tk), lambda i, j, s: (s, i)), pl.BlockSpec((tt, tn), lambda i, j, s: (s, j))],
        out_specs=pl.BlockSpec((tk, tn), lambda i, j, s: (i, j)), out_shape=_sds((k, n), F32),
        compiler_params=_params(("parallel", "parallel", "arbitrary")),
    )(a, b)


def _merge_bwd(dx1, m2, g_post, w_mo_t, gate, y_pre, o, cb, cc, cx, conv_w, w_val, w_glu, w_ao, w_co, w_val_t, w_glu_t,
               w_ao_t, w_co_t):
    t = dx1.shape[0]

    def body(dx1_ref, m2_ref, gp_ref, wmot_ref, gate_ref, y_ref, o_ref, cb_ref, cc_ref, cx_ref, ccp_ref, cxp_ref, cw_ref,
             wv_ref, wg_ref, wao_ref, wco_ref, wvt_ref, wgt_ref, waot_ref, wcot_ref,
             dgp_ref, dyp_ref, do_ref, dcb_ref, dyc_ref, dm2_ref, da_ref, dbg_ref, yg_ref, dyb_ref, dycc_ref, cy_ref,
             dgpost_ref, dbgate_ref):
        i = pl.program_id(0)

        @pl.when(i == 0)
        def _():
            dgpost_ref[...] = jnp.zeros_like(dgpost_ref)
            dbgate_ref[...] = jnp.zeros_like(dbgate_ref)

        dm2, dgpost = _rms_bwd(m2_ref[...], gp_ref[...], dx1_ref[...])
        dgpost_ref[...] += _colsum(dgpost)
        dm2b = dm2.astype(BF16)
        dm2_ref[...] = dm2b
        dmerged = _dot(dm2b, wmot_ref[...])

        yv = y_ref[...]
        ygb = _gelu(yv).astype(BF16)
        yg_ref[...] = ygb
        a_val = _dot(ygb, wv_ref[...])
        s_glu = _sigmoid(_dot(ygb, wg_ref[...]))
        y_a = a_val * s_glu
        y_b = _dot(o_ref[...], wao_ref[...])
        cbv = cb_ref[...]
        yconv = _conv_fwd(cc_ref[...] * cx_ref[...], jnp.where(i > 0, ccp_ref[...] * cxp_ref[...], 0.0), cw_ref[...])[0]
        cyb = (cbv * yconv).astype(BF16)
        cy_ref[...] = cyb
        y_c = _dot(cyb, wco_ref[...])

        gate_v = gate_ref[...]
        g_a, g_b, g_c = gate_v[:, :D_MODEL], gate_v[:, D_MODEL:2 * D_MODEL], gate_v[:, 2 * D_MODEL:]
        dgp = jnp.concatenate([dmerged * y_a * g_a * (1.0 - g_a), dmerged * y_b * g_b * (1.0 - g_b),
                               dmerged * y_c * g_c * (1.0 - g_c)], axis=1)
        dbgate_ref[...] += _colsum(dgp)
        dgp_ref[...] = dgp.astype(BF16)

        dy_a = dmerged * g_a
        d_val = (dy_a * s_glu).astype(BF16)
        d_glu = (dy_a * a_val * s_glu * (1.0 - s_glu)).astype(BF16)
        da_ref[...] = d_val
        dbg_ref[...] = d_glu
        dyp_ref[...] = (_dot(d_val, wvt_ref[...]) + _dot(d_glu, wgt_ref[...])) * _gelu_grad(yv)

        dy_b = (dmerged * g_b).astype(BF16)
        dyb_ref[...] = dy_b
        do_ref[...] = _dot(dy_b, waot_ref[...]).astype(BF16)

        dy_c = (dmerged * g_c).astype(BF16)
        dycc_ref[...] = dy_c
        dcy = _dot(dy_c, wcot_ref[...])
        dcb_ref[...] = dcy * yconv
        dyc_ref[...] = dcy * cbv

    outs = (_sds((t, D_GATE), BF16), _sds((t, D_SSM), F32), _sds((t, D_SB), BF16), _sds((t, D_CONV), F32),
            _sds((t, D_CONV), F32), _sds((t, D_MODEL), BF16), _sds((t, D_MODEL), BF16), _sds((t, D_MODEL), BF16),
            _sds((t, D_SSM), BF16), _sds((t, D_MODEL), BF16), _sds((t, D_MODEL), BF16), _sds((t, D_CONV), BF16),
            _sds((1, D_MODEL), F32), _sds((1, D_GATE), F32))
    out_specs = tuple(_rows(TM, s.shape[1]) for s in outs[:-2]) + (_whole((1, D_MODEL)), _whole((1, D_GATE)))
    return pl.pallas_call(
        body, name="merge_bwd", grid=(t // TM,),
        in_specs=[_rows(TM, D_MODEL), _rows(TM, D_MODEL), _whole((1, D_MODEL)), _whole((D_MODEL, D_MODEL)),
                  _rows(TM, D_GATE), _rows(TM, D_SSM), _rows(TM, D_SB), _rows(TM, D_CONV), _rows(TM, D_CONV),
                  _rows(TM, D_CONV), _halo_before(D_CONV), _halo_before(D_CONV), _whole((3, D_CONV)),
                  _whole((D_SSM, D_MODEL)), _whole((D_SSM, D_MODEL)), _whole((D_SB, D_MODEL)),
                  _whole((D_CONV, D_MODEL)), _whole((D_MODEL, D_SSM)), _whole((D_MODEL, D_SSM)),
                  _whole((D_MODEL, D_SB)), _whole((D_MODEL, D_CONV))],
        out_specs=out_specs, out_shape=outs, compiler_params=_params(("arbitrary",)),
    )(dx1, m2, g_post, w_mo_t, gate, y_pre, o, cb, cc, cx, cc, cx, conv_w, w_val, w_glu, w_ao, w_co, w_val_t, w_glu_t,
      w_ao_t, w_co_t)


def _conv_bwd(dyconv, cc, cx, conv_w):
    t = dyconv.shape[0]
    nt = t // TM

    def body(dy_ref, dya_ref, cc_ref, cx_ref, ccp_ref, cxp_ref, cw_ref, dcc_ref, dcx_ref, dcw_ref):
        i = pl.program_id(0)

        @pl.when(i == 0)
        def _():
            dcw_ref[...] = jnp.zeros_like(dcw_ref)

        dy = dy_ref[...]
        dy_after = jnp.where(i < nt - 1, dya_ref[...], 0.0)
        row = lax.broadcasted_iota(jnp.int32, dy.shape, 0)
        dy1 = jnp.where(row == TM - 1, dy_after[0:1, :], pltpu.roll(dy, TM - 1, axis=0))
        dy2 = jnp.where(row == TM - 1, dy_after[1:2, :],
                        jnp.where(row == TM - 2, dy_after[0:1, :], pltpu.roll(dy, TM - 2, axis=0)))
        cw = cw_ref[...]
        dz = cw[2:3, :] * dy + cw[1:2, :] * dy1 + cw[0:1, :] * dy2
        ccv, cxv = cc_ref[...], cx_ref[...]
        dcc_ref[...] = dz * cxv
        dcx_ref[...] = dz * ccv
        z = ccv * cxv
        _, z1, z2 = _conv_fwd(z, jnp.where(i > 0, ccp_ref[...] * cxp_ref[...], 0.0), cw)
        dcw_ref[0:1, :] += _colsum(dy * z2)
        dcw_ref[1:2, :] += _colsum(dy * z1)
        dcw_ref[2:3, :] += _colsum(dy * z)

    return pl.pallas_call(
        body, name="conv_bwd", grid=(nt,),
        in_specs=[_rows(TM, D_CONV), _halo_after(D_CONV, nt), _rows(TM, D_CONV), _rows(TM, D_CONV),
                  _halo_before(D_CONV), _halo_before(D_CONV), _whole((3, D_CONV))],
        out_specs=(_rows(TM, D_CONV), _rows(TM, D_CONV), _whole((8, D_CONV))),
        out_shape=(_sds((t, D_CONV), F32), _sds((t, D_CONV), F32), _sds((8, D_CONV), F32)),
        compiler_params=_params(("arbitrary",)),
    )(dyconv, dyconv, cc, cx, cc, cx, conv_w)


def _attn_bwd(q_r, q_t, do_r, do_t, k_t, k_r, v_t, r_tab):
    h, t, _ = q_r.shape
    nk = t // BQ

    def body(q_ref, qt_ref, do_ref, dot_ref, kt_ref, kr_ref, vt_ref, rtab_ref, dq_ref, dkt_ref, dvt_ref):
        i = pl.program_id(1)

        @pl.when(i == 0)
        def _():
            dkt_ref[...] = jnp.zeros_like(dkt_ref)
            dvt_ref[...] = jnp.zeros_like(dvt_ref)

        q, q_tr, do, do_tr, rtab = q_ref[0], qt_ref[0, 0], do_ref[0], dot_ref[0, 0], rtab_ref[0]
        rows = lax.broadcasted_iota(jnp.int32, (BQ, BQ), 0)
        cols = lax.broadcasted_iota(jnp.int32, (BQ, BQ), 1)
        tri = rows > cols
        incl = (rows >= cols).astype(BF16)
        lower = (rows < cols).astype(BF16)
        lane = lax.broadcasted_iota(jnp.int32, (BQ, R_LANES), 1)

        def scores(j, mask):
            nz = _dot(q, kt_ref[0, j])
            b = _sb_log1m(nz, mask)
            w = _sb_weights(nz, b, incl, _rowsum(jnp.where(lane == j, rtab, 0.0)), mask)
            return b, w, w * _dot(do, vt_ref[0, j])

        def finish(j, b, w, e, p_run, dq, mask):
            p = p_run + _dot(e.astype(BF16), lower)
            dz = (e + p) * jnp.exp(b) - p
            if mask is not None:
                dz = jnp.where(mask, dz, 0.0)
            dzb = dz.astype(BF16)
            dvt_ref[0, j] += _dot(do_tr, w.astype(BF16))
            dkt_ref[0, j] -= _dot(q_tr, dzb)
            return dq + _dot(dzb, kr_ref[0, j])

        def group(js, carry, mask=None):
            p_run, dq = carry
            parts = [scores(j, mask) for j in js]
            ps = [p_run]
            for _, _, e in parts:
                ps.append(ps[-1] + _rowsum(e))
            for j, (b, w, e), p in zip(js, parts, ps):
                dq = finish(j, b, w, e, p, dq, mask)
            return ps[-1], dq

        carry = (jnp.zeros((BQ, 1), F32), jnp.zeros((BQ, SB_HEAD_DIM), F32))
        full = i // SB_UNROLL
        carry = lax.fori_loop(0, full, lambda m, c: group([SB_UNROLL * m + k for k in range(SB_UNROLL)], c), carry)
        rem = i % SB_UNROLL
        carry = lax.fori_loop(0, rem // 2, lambda n, c: group([SB_UNROLL * full + 2 * n + k for k in range(2)], c), carry)
        carry = lax.fori_loop(0, rem % 2, lambda n, c: group([i - 1], c), carry)
        dq_ref[0] = group([i], carry, tri)[1] * SB_SCALE

    row_blk = pl.BlockSpec((1, BQ, SB_HEAD_DIM), lambda hh, i: (hh, i, 0))
    col_blk = pl.BlockSpec((1, 1, SB_HEAD_DIM, BQ), lambda hh, i: (hh, i, 0, 0))
    head_t = pl.BlockSpec((1, nk, SB_HEAD_DIM, BQ), lambda hh, i: (hh, 0, 0, 0))
    head_r = pl.BlockSpec((1, nk, BQ, SB_HEAD_DIM), lambda hh, i: (hh, 0, 0, 0))
    return pl.pallas_call(
        body, name="attn_bwd", grid=(h, nk),
        in_specs=[row_blk, col_blk, row_blk, col_blk, head_t, head_r, head_t,
                  pl.BlockSpec((1, BQ, R_LANES), lambda hh, i: (hh, i, 0))],
        out_specs=(row_blk, head_t, head_t),
        out_shape=(_sds((h, t, SB_HEAD_DIM), F32), _sds((h, nk, SB_HEAD_DIM, BQ), F32),
                   _sds((h, nk, SB_HEAD_DIM, BQ), F32)),
        compiler_params=_params(("parallel", "arbitrary")),
    )(q_r, q_t, do_r, do_t, k_t, k_r, v_t, r_tab)


def _ssm_bwd(dy, u, h_re, h_im, ct_re, ct_im, bt_re, bt_im, ab_re, ab_im, d_skip):
    t = u.shape[0]
    nt = t // TM

    def rev(n):
        return pl.BlockSpec((TM, n), lambda i: (nt - 1 - i, 0))

    def before(n):
        return pl.BlockSpec((HALO, n), lambda i: (jnp.maximum((nt - 1 - i) * (TM // HALO) - 1, 0), 0))

    def body(dy_ref, u_ref, hre_ref, him_ref, hrp_ref, hip_ref, ctre_ref, ctim_ref, btre_ref, btim_ref, ar_ref, ai_ref,
             d_ref, du_ref, dbre_ref, dbim_ref, dcre_ref, dcim_ref, dar_ref, dai_ref, dd_ref,
             ghr_ref, ghi_ref, lamr_ref, lami_ref, car_ref, cai_ref):
        i = pl.program_id(0)

        @pl.when(i == 0)
        def _():
            for ref in (car_ref, cai_ref, dbre_ref, dbim_ref, dcre_ref, dcim_ref, dar_ref, dai_ref, dd_ref):
                ref[...] = jnp.zeros_like(ref)

        dyv = dy_ref[...]
        dyb = dyv.astype(BF16)
        ghr_ref[...] = _dot(dyb, ctre_ref[...])
        ghi_ref[...] = -_dot(dyb, ctim_ref[...])
        _ssm_scan_rows(TM, ar_ref[...], ai_ref[...], -1.0, ghr_ref, ghi_ref, lamr_ref, lami_ref, car_ref, cai_ref, True)
        lam_r, lam_i = lamr_ref[...], lami_ref[...]
        lam_rb, lam_ib = lam_r.astype(BF16), lam_i.astype(BF16)
        uv = u_ref[...]
        ub = uv.astype(BF16)
        du_ref[...] = _dot(lam_rb, btre_ref[...]) + _dot(lam_ib, btim_ref[...]) + d_ref[...] * dyv
        dbre_ref[...] += _dot_tn(ub, lam_rb)
        dbim_ref[...] += _dot_tn(ub, lam_ib)
        h_r, h_i = hre_ref[...], him_ref[...]
        dcre_ref[...] += _dot_tn(dyb, h_r.astype(BF16))
        dcim_ref[...] -= _dot_tn(dyb, h_i.astype(BF16))
        first = i == nt - 1
        row = lax.broadcasted_iota(jnp.int32, h_r.shape, 0)
        h_r1 = jnp.where(row == 0, jnp.where(first, 0.0, hrp_ref[HALO - 1:HALO, :]), pltpu.roll(h_r, 1, axis=0))
        h_i1 = jnp.where(row == 0, jnp.where(first, 0.0, hip_ref[HALO - 1:HALO, :]), pltpu.roll(h_i, 1, axis=0))
        dar_ref[...] += _colsum(lam_r * h_r1 + lam_i * h_i1)
        dai_ref[...] += _colsum(lam_i * h_r1 - lam_r * h_i1)
        dd_ref[...] += _colsum(dyv * uv)

    outs = (_sds((t, D_SSM), F32), _sds((D_SSM, N_STATE), F32), _sds((D_SSM, N_STATE), F32),
            _sds((D_SSM, N_STATE), F32), _sds((D_SSM, N_STATE), F32), _sds((1, N_STATE), F32),
            _sds((1, N_STATE), F32), _sds((1, D_SSM), F32))
    return pl.pallas_call(
        body, name="ssm_bwd", grid=(nt,),
        in_specs=[rev(D_SSM), rev(D_SSM), rev(N_STATE), rev(N_STATE), before(N_STATE), before(N_STATE),
                  _whole((D_SSM, N_STATE)), _whole((D_SSM, N_STATE)), _whole((N_STATE, D_SSM)),
                  _whole((N_STATE, D_SSM)), _whole((1, N_STATE)), _whole((1, N_STATE)), _whole((1, D_SSM))],
        out_specs=(rev(D_SSM),) + tuple(_whole(s.shape) for s in outs[1:]), out_shape=outs,
        scratch_shapes=[pltpu.VMEM((TM, N_STATE), F32) for _ in range(4)]
        + [pltpu.VMEM((1, N_STATE), F32), pltpu.VMEM((1, N_STATE), F32)],
        compiler_params=_params(("arbitrary",)),
    )(dy, u, h_re, h_im, h_re, h_im, ct_re, ct_im, bt_re, bt_im, ab_re, ab_im, d_skip)


def _in_bwd(dp, dgp, w_in_t, w_gate_t, x, g_pre, dx_res):
    t = x.shape[0]

    def body(dp_ref, dgp_ref, wit_ref, wgt_ref, x_ref, g_ref, dxr_ref, dx_ref, dg_ref):
        @pl.when(pl.program_id(0) == 0)
        def _():
            dg_ref[...] = jnp.zeros_like(dg_ref)

        dh = _dot(dp_ref[...], wit_ref[...]) + _dot(dgp_ref[...], wgt_ref[...])
        dxn, dg = _rms_bwd(x_ref[...], g_ref[...], dh)
        dg_ref[...] += _colsum(dg)
        dx_ref[...] = dxr_ref[...] + dxn

    return pl.pallas_call(
        body, name="in_bwd", grid=(t // TM,),
        in_specs=[_rows(TM, D_IN), _rows(TM, D_GATE), _whole((D_IN, D_MODEL)), _whole((D_GATE, D_MODEL)),
                  _rows(TM, D_MODEL), _whole((1, D_MODEL)), _rows(TM, D_MODEL)],
        out_specs=(_rows(TM, D_MODEL), _whole((1, D_MODEL))),
        out_shape=(_sds((t, D_MODEL), F32), _sds((1, D_MODEL), F32)),
        compiler_params=_params(("arbitrary",)),
    )(dp, dgp, w_in_t, w_gate_t, x, g_pre, dx_res)


def _ssm_disc(lr, li, ldt):
    dt = jnp.exp(ldt)
    mag = jnp.exp(lr * dt)
    th = li * dt
    cs, sn = jnp.cos(th), jnp.sin(th)
    ab_re, ab_im = mag * cs, mag * sn
    den = lr * lr + li * li
    xr = ab_re - 1.0
    co_re = (xr * lr + ab_im * li) / den
    co_im = (ab_im * lr - xr * li) / den
    return dt, mag, cs, sn, ab_re, ab_im, den, xr, co_re, co_im


def _ssm_param_fwd(lr, li, ldt, b_re_t, b_im_t):
    def body(lr_ref, li_ref, ldt_ref, br_ref, bi_ref, are_ref, aim_ref, bbr_ref, bbi_ref):
        _, _, _, _, ab_re, ab_im, _, _, co_re, co_im = _ssm_disc(lr_ref[...], li_ref[...], ldt_ref[...])
        are_ref[...] = ab_re
        aim_ref[...] = ab_im
        br, bi = br_ref[...], bi_ref[...]
        bbr_ref[...] = co_re * br - co_im * bi
        bbi_ref[...] = co_re * bi + co_im * br

    vec, mat = _sds((1, N_STATE), F32), _sds((SSM_GROUP, N_STATE), F32)
    return pl.pallas_call(body, name="ssm_param_fwd", out_shape=(vec, vec, mat, mat))(lr, li, ldt, b_re_t, b_im_t)


def _ssm_param_bwd(lr, li, ldt, b_re_t, b_im_t, dab_re, dab_im, dbb_re_t, dbb_im_t):
    def body(lr_ref, li_ref, ldt_ref, br_ref, bi_ref, dar_ref, dai_ref, dbbr_ref, dbbi_ref,
             dlr_ref, dli_ref, dldt_ref, dbr_ref, dbi_ref):
        lr, li = lr_ref[...], li_ref[...]
        dt, mag, cs, sn, ab_re, ab_im, den, xr, co_re, co_im = _ssm_disc(lr, li, ldt_ref[...])
        br, bi, dbbr, dbbi = br_ref[...], bi_ref[...], dbbr_ref[...], dbbi_ref[...]
        dbr_ref[...] = co_re * dbbr + co_im * dbbi
        dbi_ref[...] = co_re * dbbi - co_im * dbbr
        dco_re = _colsum(br * dbbr + bi * dbbi)
        dco_im = _colsum(br * dbbi - bi * dbbr)
        dxr = (dco_re * lr - dco_im * li) / den
        dab_i = dai_ref[...] + (dco_re * li + dco_im * lr) / den
        dab_r = dar_ref[...] + dxr
        dden = -(co_re * dco_re + co_im * dco_im) / den
        dlr = (dco_re * xr + dco_im * ab_im) / den + dden * 2.0 * lr
        dli = (dco_re * ab_im - dco_im * xr) / den + dden * 2.0 * li
        dmag = dab_r * cs + dab_i * sn
        dth = mag * (dab_i * cs - dab_r * sn)
        dlr_ref[...] = dlr + dmag * mag * dt
        dli_ref[...] = dli + dth * dt
        dldt = jnp.broadcast_to((dmag * mag * lr + dth * li) * dt, (8, N_STATE))
        group = (lax.broadcasted_iota(jnp.int32, (N_STATE, 128), 0) // SSM_STATE
                 == lax.broadcasted_iota(jnp.int32, (N_STATE, 128), 1)).astype(F32)
        dldt_ref[...] = jnp.dot(dldt, group, precision=lax.Precision.HIGHEST, preferred_element_type=F32)

    vec, mat = _sds((1, N_STATE), F32), _sds((SSM_GROUP, N_STATE), F32)
    return pl.pallas_call(body, name="ssm_param_bwd", out_shape=(vec, vec, _sds((8, 128), F32), mat, mat))(
        lr, li, ldt, b_re_t, b_im_t, dab_re, dab_im, dbb_re_t, dbb_im_t)


ANY = pl.BlockSpec(memory_space=pl.ANY)


def _chip_peers(x, y):
    return [(1 - x, y), (x, 1 - y), (1 - x, 1 - y)]


def _gather_chips(buf):
    half = buf.shape[0] // 2

    def body(src_ref, out_ref, send_sems, recv_sems, pass_send_sems, pass_recv_sems, local_sem):
        x, y, c = lax.axis_index("x"), lax.axis_index("y"), lax.axis_index("c")
        me, sibling = 2 * x + y, (x, y, 1 - c)
        mine, theirs = pl.ds(c * half, half), pl.ds((1 - c) * half, half)
        peers = _chip_peers(x, y)
        local = pltpu.make_async_copy(src_ref, out_ref.at[me], local_sem)
        local.start()
        copies = []
        for k, (px, py) in enumerate(peers):
            copies.append(pltpu.make_async_remote_copy(src_ref.at[mine], out_ref.at[me, mine], send_sems.at[k],
                                                       recv_sems.at[k], device_id=(px, py, c), device_id_type=MESH))
            copies[-1].start()
        for k, (px, py) in enumerate(peers):
            landed = out_ref.at[2 * px + py, mine]
            pltpu.make_async_remote_copy(src_ref.at[mine], landed, send_sems.at[k], recv_sems.at[k],
                                         device_id=(px, py, c), device_id_type=MESH).wait_recv()
            copies.append(pltpu.make_async_remote_copy(landed, landed, pass_send_sems.at[k], pass_recv_sems.at[k],
                                                       device_id=sibling, device_id_type=MESH))
            copies[-1].start()
        for k, (px, py) in enumerate(peers):
            other = out_ref.at[2 * px + py, theirs]
            pltpu.make_async_remote_copy(other, other, pass_send_sems.at[k], pass_recv_sems.at[k], device_id=sibling,
                                         device_id_type=MESH).wait_recv()
        for cp in copies:
            cp.wait_send()
        local.wait()

    return pl.pallas_call(
        body, name="gather_chips", in_specs=[ANY], out_specs=ANY, out_shape=_sds((N_CHIPS,) + buf.shape, buf.dtype),
        scratch_shapes=[pltpu.SemaphoreType.DMA((3,)) for _ in range(4)] + [pltpu.SemaphoreType.DMA],
    )(buf)


def _scatter_chips(buf):
    def body(src_ref, out_ref, send_sems, recv_sems, local_sem):
        x, y, c = lax.axis_index("x"), lax.axis_index("y"), lax.axis_index("c")
        me = 2 * x + y
        local = pltpu.make_async_copy(src_ref.at[me], out_ref.at[me], local_sem)
        local.start()
        sends = []
        for k, (px, py) in enumerate(_chip_peers(x, y)):
            sends.append(pltpu.make_async_remote_copy(src_ref.at[2 * px + py], out_ref.at[me], send_sems.at[k],
                                                      recv_sems.at[k], device_id=(px, py, c), device_id_type=MESH))
            sends[-1].start()
        for k, (px, py) in enumerate(_chip_peers(x, y)):
            pltpu.make_async_remote_copy(src_ref.at[me], out_ref.at[2 * px + py], send_sems.at[k], recv_sems.at[k],
                                         device_id=(px, py, c), device_id_type=MESH).wait_recv()
        for cp in sends:
            cp.wait_send()
        local.wait()

    return pl.pallas_call(
        body, name="scatter_chips", in_specs=[ANY], out_specs=ANY, out_shape=_sds(buf.shape, buf.dtype),
        scratch_shapes=[pltpu.SemaphoreType.DMA((3,)), pltpu.SemaphoreType.DMA((3,)), pltpu.SemaphoreType.DMA],
    )(buf)


SWAP_CHUNKS = 8


def _swap_cores(buf):
    rows = buf.shape[0] // SWAP_CHUNKS

    def body(src_ref, out_ref, send_sems, recv_sems):
        x, y, c = lax.axis_index("x"), lax.axis_index("y"), lax.axis_index("c")
        copies = []
        for k in range(SWAP_CHUNKS):
            chunk = pl.ds(k * rows, rows)
            copies.append(pltpu.make_async_remote_copy(src_ref.at[chunk], out_ref.at[chunk], send_sems.at[k],
                                                       recv_sems.at[k], device_id=(x, y, 1 - c), device_id_type=MESH))
            copies[-1].start()
        for cp in copies:
            cp.wait_recv()
        for cp in copies:
            cp.wait_send()

    return pl.pallas_call(
        body, name="swap_cores", in_specs=[ANY], out_specs=ANY, out_shape=_sds(buf.shape, buf.dtype),
        scratch_shapes=[pltpu.SemaphoreType.DMA((SWAP_CHUNKS,)), pltpu.SemaphoreType.DMA((SWAP_CHUNKS,))],
    )(buf)


def _gather_all(buf):
    flips = [(dx, dy, dc) for dx in (0, 1) for dy in (0, 1) for dc in (0, 1)][1:]

    def body(src_ref, out_ref, send_sems, recv_sems, local_sem):
        x, y, c = lax.axis_index("x"), lax.axis_index("y"), lax.axis_index("c")
        me = 4 * x + 2 * y + c
        peers = [(1 - x if dx else x, 1 - y if dy else y, 1 - c if dc else c) for dx, dy, dc in flips]
        local = pltpu.make_async_copy(src_ref, out_ref.at[me], local_sem)
        local.start()
        sends = []
        for k, peer in enumerate(peers):
            sends.append(pltpu.make_async_remote_copy(src_ref, out_ref.at[me], send_sems.at[k], recv_sems.at[k],
                                                      device_id=peer, device_id_type=MESH))
            sends[-1].start()
        for k, (px, py, pc) in enumerate(peers):
            pltpu.make_async_remote_copy(src_ref, out_ref.at[4 * px + 2 * py + pc], send_sems.at[k], recv_sems.at[k],
                                         device_id=(px, py, pc), device_id_type=MESH).wait_recv()
        for cp in sends:
            cp.wait_send()
        local.wait()

    return pl.pallas_call(
        body, name="gather_all", in_specs=[ANY], out_specs=ANY, out_shape=_sds((N_DEV,) + buf.shape, buf.dtype),
        scratch_shapes=[pltpu.SemaphoreType.DMA((7,)), pltpu.SemaphoreType.DMA((7,)), pltpu.SemaphoreType.DMA],
    )(buf)


PACK_W = 1024
PACK_ROWS = 256


def _sum_parts(parts):
    n, r, w = parts.shape

    def body(p_ref, o_ref):
        acc = p_ref[0].astype(F32)
        for k in range(1, n):
            acc = acc + p_ref[k].astype(F32)
        o_ref[...] = acc

    return pl.pallas_call(
        body, name="sum_parts", grid=(r // PACK_ROWS,),
        in_specs=[pl.BlockSpec((n, PACK_ROWS, w), lambda i: (0, i, 0))], out_specs=_rows(PACK_ROWS, w),
        out_shape=_sds((r, w), F32), compiler_params=_params(("parallel",)),
    )(parts)


def _adamw(parts, w, m, v):
    n = len(parts)
    r, wd = w.shape

    def body(*refs):
        w_ref, m_ref, v_ref, g_ref, dw_ref, nm_ref, nv_ref = refs[n:]
        g = refs[0][...]
        for k in range(1, n):
            g = g + refs[k][...]
        g_ref[...] = g
        m_new = ADAM_B1 * m_ref[...] + (1.0 - ADAM_B1) * g
        v_new = ADAM_B2 * v_ref[...] + (1.0 - ADAM_B2) * (g * g)
        nm_ref[...] = m_new
        nv_ref[...] = v_new
        m_hat = m_new / (1.0 - ADAM_B1 ** ADAM_STEP)
        v_hat = v_new / (1.0 - ADAM_B2 ** ADAM_STEP)
        dw_ref[...] = -ADAM_LR * (m_hat / (jnp.sqrt(v_hat) + ADAM_EPS) + ADAM_WD * w_ref[...])

    blk = _rows(PACK_ROWS, wd)
    return pl.pallas_call(
        body, name="adamw", grid=(r // PACK_ROWS,),
        in_specs=[blk] * (n + 3), out_specs=(blk,) * 4,
        out_shape=(_sds((r, wd), F32),) * 4, compiler_params=_params(("parallel",)),
    )(*parts, w, m, v)


PACK_UNIT = PACK_W * 16


def _pack(arrs, dtype):
    unit = PACK_UNIT
    flat = []
    for a in arrs:
        a = a.reshape(-1).astype(dtype)
        flat.append(jnp.pad(a, (0, -a.size % unit)))
    total = sum(f.size for f in flat)
    flat.append(jnp.zeros((-total % (PACK_W * PACK_ROWS),), dtype))
    return jnp.concatenate(flat).reshape(-1, PACK_W)


def _unpack(buf, shapes):
    unit = PACK_UNIT
    flat = buf.reshape(-1)
    out, off = [], 0
    for shp in shapes:
        size = 1
        for s in shp:
            size *= s
        out.append(flat[off:off + size].reshape(shp))
        off += size + (-size % unit)
    return out


def _expand_bd(m_t):
    g_row = jnp.arange(D_SSM)[:, None] // SSM_GROUP
    g_col = jnp.arange(N_STATE)[None, :] // SSM_STATE
    return jnp.where(g_row == g_col, jnp.tile(m_t, (SSM_GROUPS, 1)), 0.0)


def _extract_bd(full):
    g_row = jnp.arange(D_SSM)[:, None] // SSM_GROUP
    g_col = jnp.arange(N_STATE)[None, :] // SSM_STATE
    return jnp.where(g_row == g_col, full, 0.0).reshape(SSM_GROUPS, SSM_GROUP, N_STATE).sum(0)


def _heads_rows(a):
    return a.reshape(a.shape[0], SB_HEADS, SB_HEAD_DIM).transpose(1, 0, 2)


def _heads_blocks_t(a):
    return a.reshape(a.shape[0] // BQ, BQ, SB_HEADS, SB_HEAD_DIM).transpose(2, 0, 3, 1)


def _heads_blocks_r(a):
    return a.reshape(a.shape[0] // BQ, BQ, SB_HEADS, SB_HEAD_DIM).transpose(2, 0, 1, 3)


def _prep_layer(w, l):
    p = {}
    for name in ('norm_mix_pre', 'norm_mix_post', 'b_gate', 'norm_ffn_pre', 'norm_ffn_post', 'ssm_d'):
        p[name] = w[name][l][None, :]
    for name in ('w_in', 'w_gate', 'w_glu_val', 'w_glu_gate', 'w_attn_out', 'w_conv_out', 'w_mix_out', 'w_ffn_up',
                 'w_ffn_down'):
        p[name] = w[name][l]
        p[name + '_t'] = w[name][l].T
    p['conv_w'] = w['conv_w'][l]
    p['lr'] = w['ssm_a_re'][l].reshape(1, N_STATE)
    p['li'] = w['ssm_a_im'][l].reshape(1, N_STATE)
    p['ldt'] = jnp.repeat(w['ssm_log_dt'][l], SSM_STATE).reshape(1, N_STATE)
    p['b_re_t'] = w['ssm_b_re'][l].transpose(2, 0, 1).reshape(SSM_GROUP, N_STATE)
    p['b_im_t'] = w['ssm_b_im'][l].transpose(2, 0, 1).reshape(SSM_GROUP, N_STATE)
    p['ab_re'], p['ab_im'], bb_re_t, bb_im_t = _ssm_param_fwd(p['lr'], p['li'], p['ldt'], p['b_re_t'], p['b_im_t'])
    p['bb_re'] = _expand_bd(bb_re_t).astype(BF16)
    p['bb_im'] = _expand_bd(bb_im_t).astype(BF16)
    p['ct_re'] = _expand_bd(w['ssm_c_re'][l].transpose(1, 0, 2).reshape(SSM_GROUP, N_STATE)).astype(BF16)
    p['ct_im'] = _expand_bd(w['ssm_c_im'][l].transpose(1, 0, 2).reshape(SSM_GROUP, N_STATE)).astype(BF16)
    return p


def _layer_fwd(x, p):
    s = {'x': x}
    s['hb'], s['u'], q, k, v, s['cb'], s['cc'], s['cx'], s['gate'] = _fwd_in(
        x, p['norm_mix_pre'], p['w_in'], p['w_gate'], p['b_gate'])
    s['h_re'], s['h_im'], s['y_pre'] = _ssm_fwd(s['u'], p['bb_re'], p['bb_im'], p['ab_re'], p['ab_im'], p['ct_re'].T,
                                                p['ct_im'].T, p['ssm_d'])
    s['q'], s['k'], s['v'] = q, k, v
    o_r, s['r_tab'] = _attn_fwd(_heads_rows(q), _heads_blocks_t(k), _heads_blocks_r(v))
    s['o'] = o_r.transpose(1, 0, 2).reshape(x.shape[0], D_SB).astype(BF16)
    s['x1'], s['merged'], s['m2'] = _merge_fwd(
        s['y_pre'], s['o'], s['cb'], s['cc'], s['cx'], p['conv_w'], s['gate'], x, p['w_glu_val'], p['w_glu_gate'],
        p['w_attn_out'], p['w_conv_out'], p['w_mix_out'], p['norm_mix_post'])
    x2, s['h2'], s['f'] = _ffn_fwd(s['x1'], p['norm_ffn_pre'], p['norm_ffn_post'], p['w_ffn_up'], p['w_ffn_down'])
    return x2, s


def _layer_bwd(dx2, p, s):
    t = dx2.shape[0]
    g = {}
    dx1, da, r, df, g['norm_ffn_pre'], g['norm_ffn_post'] = _ffn_bwd(
        dx2, s['f'], s['h2'], s['x1'], p['norm_ffn_pre'], p['norm_ffn_post'], p['w_ffn_up'], p['w_ffn_up_t'],
        p['w_ffn_down_t'])
    g['w_ffn_up'] = _mm_tn(s['h2'], da)
    g['w_ffn_down'] = _mm_tn(r, df)

    (dgp, dy_pre, do, dcb, dyconv, dm2, d_val, d_glu, yg, dy_b, dy_c, cy, g['norm_mix_post'], g['b_gate']) = _merge_bwd(
        dx1, s['m2'], p['norm_mix_post'], p['w_mix_out_t'], s['gate'], s['y_pre'], s['o'], s['cb'], s['cc'], s['cx'],
        p['conv_w'], p['w_glu_val'], p['w_glu_gate'], p['w_attn_out'], p['w_conv_out'], p['w_glu_val_t'],
        p['w_glu_gate_t'], p['w_attn_out_t'], p['w_conv_out_t'])
    g['w_mix_out'] = _mm_tn(s['merged'], dm2)
    g['w_glu_val'] = _mm_tn(yg, d_val)
    g['w_glu_gate'] = _mm_tn(yg, d_glu)
    g['w_attn_out'] = _mm_tn(s['o'], dy_b)
    g['w_conv_out'] = _mm_tn(cy, dy_c)

    dcc, dcx, dcw = _conv_bwd(dyconv, s['cc'], s['cx'], p['conv_w'])
    g['conv_w'] = dcw[0:3]

    dq_r, dk_t, dv_t = _attn_bwd(_heads_rows(s['q']), _heads_blocks_t(s['q']), _heads_rows(do), _heads_blocks_t(do),
                                 _heads_blocks_t(s['k']), _heads_blocks_r(s['k']), _heads_blocks_t(s['v']), s['r_tab'])
    dq = dq_r.transpose(1, 0, 2).reshape(t, D_SB)
    dk = dk_t.transpose(1, 3, 0, 2).reshape(t, D_SB)
    dv = dv_t.transpose(1, 3, 0, 2).reshape(t, D_SB)

    du, dbb_re, dbb_im, dct_re, dct_im, dab_re, dab_im, g['ssm_d'] = _ssm_bwd(
        dy_pre, s['u'], s['h_re'], s['h_im'], p['ct_re'], p['ct_im'], p['bb_re'].T, p['bb_im'].T, p['ab_re'],
        p['ab_im'], p['ssm_d'])
    dlr, dli, dldt, db_re_t, db_im_t = _ssm_param_bwd(p['lr'], p['li'], p['ldt'], p['b_re_t'], p['b_im_t'], dab_re,
                                                      dab_im, _extract_bd(dbb_re), _extract_bd(dbb_im))
    g['ssm_a_re'] = dlr.reshape(SSM_GROUPS, SSM_STATE)
    g['ssm_a_im'] = dli.reshape(SSM_GROUPS, SSM_STATE)
    g['ssm_log_dt'] = dldt[0, :SSM_GROUPS]
    g['ssm_b_re'] = db_re_t.reshape(SSM_GROUP, SSM_GROUPS, SSM_STATE).transpose(1, 2, 0)
    g['ssm_b_im'] = db_im_t.reshape(SSM_GROUP, SSM_GROUPS, SSM_STATE).transpose(1, 2, 0)
    g['ssm_c_re'] = _extract_bd(dct_re).reshape(SSM_GROUP, SSM_GROUPS, SSM_STATE).transpose(1, 0, 2)
    g['ssm_c_im'] = _extract_bd(dct_im).reshape(SSM_GROUP, SSM_GROUPS, SSM_STATE).transpose(1, 0, 2)

    dp = jnp.concatenate([du.astype(BF16), dq.astype(BF16), dk.astype(BF16), dv.astype(BF16), dcb.astype(BF16),
                          dcc.astype(BF16), dcx.astype(BF16)], axis=1)
    dx, g['norm_mix_pre'] = _in_bwd(dp, dgp, p['w_in_t'], p['w_gate_t'], s['x'], p['norm_mix_pre'], dx1)
    g['w_in'] = _mm_tn(s['hb'], dp)
    g['w_gate'] = _mm_tn(s['hb'], dgp)
    for name in ('norm_mix_pre', 'norm_mix_post', 'b_gate', 'norm_ffn_pre', 'norm_ffn_post', 'ssm_d'):
        g[name] = g[name][0]
    return dx, g


def _local_step(x, target, w):
    ps, saved = [], []
    for l in range(DEPTH):
        ps.append(_prep_layer(w, l))
        x, s = _layer_fwd(x, ps[l])
        saved.append(s)
    dx, loss_blk = _loss_grad(x, target)
    grads = [None] * DEPTH
    for l in reversed(range(DEPTH)):
        dx, grads[l] = _layer_bwd(dx, ps[l], saved[l])
    return loss_blk, dx, {n: jnp.stack([grads[l][n] for l in range(DEPTH)]) for n in WEIGHTS}


def kernel(x, norm_mix_pre, norm_mix_post, w_in, w_gate, b_gate, ssm_a_re, ssm_a_im, ssm_log_dt, ssm_b_re,
           ssm_b_im, ssm_c_re, ssm_c_im, ssm_d, w_glu_val, w_glu_gate, w_attn_out, conv_w, w_conv_out,
           w_mix_out, norm_ffn_pre, norm_ffn_post, w_ffn_up, w_ffn_down, loss_target, m_norm_mix_pre,
           m_norm_mix_post, m_w_in, m_w_gate, m_b_gate, m_ssm_a_re, m_ssm_a_im, m_ssm_log_dt, m_ssm_b_re,
           m_ssm_b_im, m_ssm_c_re, m_ssm_c_im, m_ssm_d, m_w_glu_val, m_w_glu_gate, m_w_attn_out, m_conv_w,
           m_w_conv_out, m_w_mix_out, m_norm_ffn_pre, m_norm_ffn_post, m_w_ffn_up, m_w_ffn_down,
           v_norm_mix_pre, v_norm_mix_post, v_w_in, v_w_gate, v_b_gate, v_ssm_a_re, v_ssm_a_im, v_ssm_log_dt,
           v_ssm_b_re, v_ssm_b_im, v_ssm_c_re, v_ssm_c_im, v_ssm_d, v_w_glu_val, v_w_glu_gate, v_w_attn_out,
           v_conv_w, v_w_conv_out, v_w_mix_out, v_norm_ffn_pre, v_norm_ffn_post, v_w_ffn_up, v_w_ffn_down):
    given = dict(locals())
    shard = {n: given[n] for n in WEIGHTS}
    big = list(SHARDED)

    sent = [lax.bitcast_convert_type(shard[n], BF16) if n == 'conv_w' else shard[n].astype(BF16) for n in big]
    gathered = _gather_chips(_pack(sent, BF16))
    full = {n: shard[n] for n in REPLICATED}
    per_chip = [_unpack(gathered[j], [a.shape for a in sent]) for j in range(N_CHIPS)]
    for i, n in enumerate(big):
        parts = [per_chip[j][i] for j in range(N_CHIPS)]
        if n == 'conv_w':
            parts = [lax.bitcast_convert_type(a, F32) for a in parts]
        full[n] = jnp.concatenate(parts, axis=SHARDED[n])

    loss_blk, dx, grads = _local_step(x[0], loss_target[0], full)
    loss = lax.psum(loss_blk[0, 0], ("x", "y", "c"))

    to_chip = [_pack([jnp.split(grads[n], N_CHIPS, axis=SHARDED[n])[j] for n in big], BF16) for j in range(N_CHIPS)]
    core_sum = _sum_parts(_scatter_chips(jnp.stack(to_chip)))
    out_big = _adamw([core_sum, _swap_cores(core_sum)], _pack([shard[n] for n in big], F32),
                     _pack([given['m_' + n] for n in big], F32), _pack([given['v_' + n] for n in big], F32))
    all_parts = _gather_all(_pack([grads[n] for n in REPLICATED], F32))
    out_small = _adamw([all_parts[k] for k in range(N_DEV)], _pack([shard[n] for n in REPLICATED], F32),
                       _pack([given['m_' + n] for n in REPLICATED], F32),
                       _pack([given['v_' + n] for n in REPLICATED], F32))

    results = []
    for kind in range(4):
        got = dict(zip(big, _unpack(out_big[kind], [shard[n].shape for n in big])))
        got.update(zip(REPLICATED, _unpack(out_small[kind], [shard[n].shape for n in REPLICATED])))
        results += [got[n] for n in WEIGHTS]
    return (loss, dx[None], *results)
```
